```python
import jax, jax.numpy as jnp
from jax import lax
import numpy as np

D_MODEL = 1024
BATCH = 16
SEQ = 2048
DEPTH = 1
DEC_BATCH = 8
DEC_SEQ = 2048
PAST_LEN = 128

GRID_W = 64
N_MEM = 256
EPS = 1e-6
GLA_HEADS = 4
GLA_DK = D_MODEL // 2 // GLA_HEADS
GLA_DV = D_MODEL // GLA_HEADS
GLA_QK = GLA_HEADS * GLA_DK
GLA_V = GLA_HEADS * GLA_DV
GLA_RANK = 16
GLA_TAU = 16.0
GLA_CHUNK = 64
ATT_HEADS = 8
ATT_KV_HEADS = 2
ATT_HD = 128
ATT_Q = ATT_HEADS * ATT_HD
ATT_KV = ATT_KV_HEADS * ATT_HD
ROPE_THETA = 10000.0
Q_BLOCK = 128
X_HEADS = 4
X_HD = D_MODEL // X_HEADS
D_FF = ((8 * D_MODEL + 3 * 256 - 1) // (3 * 256)) * 256
IN_SPLITS = (GLA_QK, GLA_QK, GLA_V, GLA_V, GLA_RANK, GLA_RANK, ATT_Q, ATT_KV, ATT_KV, D_MODEL, D_MODEL)
D_IN = sum(IN_SPLITS)

kernel_name = "hybrid_gla_gqa_axial_encoder"


def rmsnorm(x, w):
    xf = x.astype(jnp.float32)
    y = xf * lax.rsqrt(jnp.mean(xf * xf, axis=-1, keepdims=True) + EPS) * w.astype(jnp.float32)
    return y.astype(x.dtype)


def gla_chunked(q, k, v, g, strict):
    B, H, T, dk = q.shape
    dv = v.shape[-1]
    C = GLA_CHUNK
    N = T // C
    q = q.reshape(B, H, N, C, dk)
    k = k.reshape(B, H, N, C, dk)
    v = v.reshape(B, H, N, C, dv)
    b = jnp.cumsum(g.reshape(B, H, N, C, dk), axis=3)
    b_last = b[:, :, :, -1:, :]
    qg = q * jnp.exp(b)
    kg = k * jnp.exp(-b)
    k_end = k * jnp.exp(b_last - b)
    mask = jnp.tril(jnp.ones((C, C), dtype=bool), k=-1 if strict else 0)
    a = jnp.where(mask, jnp.einsum('bhnid,bhnjd->bhnij', qg, kg), 0.0)
    o_intra = jnp.einsum('bhnij,bhnje->bhnie', a, v)

    def step(S, inp):
        qg_n, kend_n, v_n, dec_n = inp
        o_n = jnp.einsum('bhid,bhde->bhie', qg_n, S)
        S = dec_n[..., None] * S + jnp.einsum('bhjd,bhje->bhde', kend_n, v_n)
        return S, o_n

    xs = (jnp.moveaxis(qg, 2, 0), jnp.moveaxis(k_end, 2, 0), jnp.moveaxis(v, 2, 0),
          jnp.moveaxis(jnp.exp(b_last[:, :, :, 0, :]), 2, 0))
    S0 = jnp.zeros((B, H, dk, dv), q.dtype)
    _, o_inter = lax.scan(step, S0, xs)
    o = o_intra + jnp.moveaxis(o_inter, 0, 2)
    return o.reshape(B, H, T, dv)


def gla_branch(q, k, v, r, zf, zb, wa_f, ba_f, wa_b, ba_b, gnorm):
    B, T, _ = q.shape
    f32 = jnp.float32

    def heads(t, d):
        return t.astype(f32).reshape(B, T, GLA_HEADS, d).transpose(0, 2, 1, 3)

    qh = heads(q, GLA_DK) * (GLA_DK ** -0.5)
    kh = heads(k, GLA_DK)
    vh = heads(v, GLA_DV)
    gf = heads(jax.nn.log_sigmoid(zf.astype(f32) @ wa_f.astype(f32) + ba_f.astype(f32)) / GLA_TAU, GLA_DK)
    gb = heads(jax.nn.log_sigmoid(zb.astype(f32) @ wa_b.astype(f32) + ba_b.astype(f32)) / GLA_TAU, GLA_DK)
    flip = lambda t: jnp.flip(t, axis=2)
    o_fwd = gla_chunked(qh, kh, vh, gf, False)
    o_bwd = flip(gla_chunked(flip(qh), flip(kh), flip(vh), flip(gb), True))
    o = rmsnorm(o_fwd + o_bwd, gnorm)
    o = o.transpose(0, 2, 1, 3).reshape(B, T, GLA_V)
    return (o * jax.nn.silu(r.astype(f32))).astype(q.dtype)


def axial_rope_tables(T):
    rows = T // GRID_W
    row = jnp.broadcast_to(jnp.arange(rows)[:, None], (rows, GRID_W)).reshape(-1).astype(jnp.float32)
    col = jnp.broadcast_to(jnp.arange(GRID_W)[None, :], (rows, GRID_W)).reshape(-1).astype(jnp.float32)
    half = ATT_HD // 2
    inv_freq = ROPE_THETA ** (-jnp.arange(0, half, 2, dtype=jnp.float32) / half)
    ang_r = row[:, None] * inv_freq[None, :]
    ang_c = col[:, None] * inv_freq[None, :]
    return jnp.cos(ang_r), jnp.sin(ang_r), jnp.cos(ang_c), jnp.sin(ang_c)


def rotate(x, cos, sin):
    shape = (cos.shape[0],) + (1,) * (x.ndim - 3) + (cos.shape[-1],)
    cos = cos.reshape(shape)
    sin = sin.reshape(shape)
    x1, x2 = jnp.split(x, 2, axis=-1)
    return jnp.concatenate([x1 * cos - x2 * sin, x2 * cos + x1 * sin], axis=-1)


def apply_axial_rope(x, tables):
    cr, sr, cc, sc = tables
    xr, xc = jnp.split(x, 2, axis=-1)
    return jnp.concatenate([rotate(xr, cr, sr), rotate(xc, cc, sc)], axis=-1)


def gqa_branch(q, k, v, q_norm, k_norm):
    B, T, _ = q.shape
    G = ATT_HEADS // ATT_KV_HEADS
    f32 = jnp.float32
    qh = rmsnorm(q.astype(f32).reshape(B, T, ATT_KV_HEADS, G, ATT_HD), q_norm)
    kh = rmsnorm(k.astype(f32).reshape(B, T, ATT_KV_HEADS, ATT_HD), k_norm)
    vh = v.astype(f32).reshape(B, T, ATT_KV_HEADS, ATT_HD)
    tables = axial_rope_tables(T)
    qh = apply_axial_rope(qh, tables) * (ATT_HD ** -0.5)
    kh = apply_axial_rope(kh, tables)
    qb = qh.reshape(B, T // Q_BLOCK, Q_BLOCK, ATT_KV_HEADS, G, ATT_HD).transpose(1, 0, 2, 3, 4, 5)

    def block(qi):
        s = jnp.einsum('bqkgd,bskd->bkgqs', qi, kh)
        p = jax.nn.softmax(s, axis=-1)
        return jnp.einsum('bkgqs,bskd->bqkgd', p, vh)

    o = lax.map(block, qb)
    o = o.transpose(1, 0, 2, 3, 4, 5).reshape(B, T, ATT_Q)
    return o.astype(q.dtype)


def cross_attn(h, m, wq, wkv, wo):
    B, T, _ = h.shape
    M = m.shape[1]
    f32 = jnp.float32
    q = (h @ wq).astype(f32).reshape(B, T, X_HEADS, X_HD) * (X_HD ** -0.5)
    k, v = jnp.split((m @ wkv).astype(f32), 2, axis=-1)
    k = k.reshape(B, M, X_HEADS, X_HD)
    v = v.reshape(B, M, X_HEADS, X_HD)
    p = jax.nn.softmax(jnp.einsum('bthd,bmhd->bhtm', q, k), axis=-1)
    o = jnp.einsum('bhtm,bmhd->bthd', p, v).reshape(B, T, D_MODEL)
    return o.astype(h.dtype) @ wo


def encoder_layer(x, mem, ln_mix_pre, w_in, gla_wa_f, gla_ba_f, gla_wa_b, gla_ba_b, gla_norm,
                  att_q_norm, att_k_norm, w_branch_gla, w_branch_att, w_out, ln_mix_post,
                  ln_x_pre, ln_mem, x_wq, x_wkv, x_wo, ln_x_post,
                  ln_ffn_pre, ffn_wi, ffn_wo, ln_ffn_post):
    h = rmsnorm(x, ln_mix_pre)
    proj = h @ w_in
    split_idx = np.cumsum(IN_SPLITS)[:-1].tolist()
    gq, gk, gv, gr, zf, zb, aq, ak, av, gate_a, gate_b = jnp.split(proj, split_idx, axis=-1)
    out_a = gla_branch(gq, gk, gv, gr, zf, zb, gla_wa_f, gla_ba_f, gla_wa_b, gla_ba_b, gla_norm) @ w_branch_gla
    out_b = gqa_branch(aq, ak, av, att_q_norm, att_k_norm) @ w_branch_att
    mixed = jax.nn.sigmoid(gate_a) * out_a + jax.nn.sigmoid(gate_b) * out_b
    x = x + rmsnorm(mixed @ w_out, ln_mix_post)
    h = rmsnorm(x, ln_x_pre)
    m = rmsnorm(mem, ln_mem)
    x = x + rmsnorm(cross_attn(h, m, x_wq, x_wkv, x_wo), ln_x_post)
    h = rmsnorm(x, ln_ffn_pre)
    g, u = jnp.split(h @ ffn_wi, 2, axis=-1)
    x = x + rmsnorm((jax.nn.silu(g) * u) @ ffn_wo, ln_ffn_post)
    return x


def setup_inputs(seed: int = 0) -> dict:
    key = jax.random.key(seed)
    ks = iter(jax.random.split(key, 40))
    f32 = jnp.float32

    def w(shape, fan_in):
        return jax.random.normal(next(ks), (DEPTH,) + shape, f32) * (fan_in ** -0.5)

    def gain(n):
        return 1.0 + 0.05 * jax.random.normal(next(ks), (DEPTH, n), f32)

    def bias(n):
        return 0.01 * jax.random.normal(next(ks), (DEPTH, n), f32)

    return {
        "x_prompt": jax.random.normal(next(ks), (BATCH, SEQ, D_MODEL), f32),
        "x_sample": jax.random.normal(next(ks), (DEC_BATCH, DEC_SEQ, D_MODEL), f32),
        "mem_prompt": jax.random.normal(next(ks), (BATCH, N_MEM, D_MODEL), f32),
        "mem_sample": jax.random.normal(next(ks), (DEC_BATCH, N_MEM, D_MODEL), f32),
        "ln_mix_pre": gain(D_MODEL),
        "w_in": w((D_MODEL, D_IN), D_MODEL),
        "gla_wa_f": w((GLA_RANK, GLA_QK), GLA_RANK),
        "gla_ba_f": bias(GLA_QK),
        "gla_wa_b": w((GLA_RANK, GLA_QK), GLA_RANK),
        "gla_ba_b": bias(GLA_QK),
        "gla_norm": gain(GLA_DV),
        "att_q_norm": gain(ATT_HD),
        "att_k_norm": gain(ATT_HD),
        "w_branch_gla": w((GLA_V, D_MODEL), GLA_V),
        "w_branch_att": w((ATT_Q, D_MODEL), ATT_Q),
        "w_out": w((D_MODEL, D_MODEL), D_MODEL),
        "ln_mix_post": gain(D_MODEL),
        "ln_x_pre": gain(D_MODEL),
        "ln_mem": gain(D_MODEL),
        "x_wq": w((D_MODEL, D_MODEL), D_MODEL),
        "x_wkv": w((D_MODEL, 2 * D_MODEL), D_MODEL),
        "x_wo": w((D_MODEL, D_MODEL), D_MODEL),
        "ln_x_post": gain(D_MODEL),
        "ln_ffn_pre": gain(D_MODEL),
        "ffn_wi": w((D_MODEL, 2 * D_FF), D_MODEL),
        "ffn_wo": w((D_FF, D_MODEL), D_FF),
        "ln_ffn_post": gain(D_MODEL),
    }


def reference(x_prompt, x_sample, mem_prompt, mem_sample, ln_mix_pre, w_in, gla_wa_f, gla_ba_f,
              gla_wa_b, gla_ba_b, gla_norm, att_q_norm, att_k_norm, w_branch_gla, w_branch_att,
              w_out, ln_mix_post, ln_x_pre, ln_mem, x_wq, x_wkv, x_wo, ln_x_post,
              ln_ffn_pre, ffn_wi, ffn_wo, ln_ffn_post):
    params = (ln_mix_pre, w_in, gla_wa_f, gla_ba_f, gla_wa_b, gla_ba_b, gla_norm,
              att_q_norm, att_k_norm, w_branch_gla, w_branch_att, w_out, ln_mix_post,
              ln_x_pre, ln_mem, x_wq, x_wkv, x_wo, ln_x_post,
              ln_ffn_pre, ffn_wi, ffn_wo, ln_ffn_post)
    y_prompt = x_prompt
    y_sample = x_sample
    for l in range(DEPTH):
        lp = [p[l] for p in params]
        y_prompt = encoder_layer(y_prompt, mem_prompt, *lp)
        y_sample = encoder_layer(y_sample, mem_sample, *lp)
    return (y_prompt, y_sample)
```

```python
import functools

import jax
import jax.numpy as jnp
from jax import lax
from jax.experimental import pallas as pl
from jax.experimental.pallas import tpu as pltpu

F32 = jnp.float32
BF16 = jnp.bfloat16

D_MODEL = 1024
GRID_W = 64
EPS = 1e-6
GLA_HEADS = 4
GLA_DK = 128
GLA_DV = 256
GLA_QK = GLA_HEADS * GLA_DK
GLA_V = GLA_HEADS * GLA_DV
GLA_RANK = 16
GLA_TAU = 16.0
GLA_CHUNK = 64
ATT_HEADS = 8
ATT_KV_HEADS = 2
ATT_GROUP = ATT_HEADS // ATT_KV_HEADS
ATT_HD = 128
ATT_Q = ATT_HEADS * ATT_HD
ATT_KV = ATT_KV_HEADS * ATT_HD
ROPE_THETA = 10000.0
Q_BLOCK = 128
X_HEADS = 4
X_HD = D_MODEL // X_HEADS
D_FF = 2816

LANE = 128
VMEM_LIMIT = 56 * 1024 * 1024

C_GQK = 0
C_GV = C_GQK + 2 * GLA_QK
C_GR = C_GV + GLA_V
C_AQ = C_GR + GLA_V
C_AK = C_AQ + ATT_Q
C_AV = C_AK + ATT_KV
C_GATE = C_AV + ATT_KV
C_Z = C_GATE + 2 * D_MODEL
C_END = C_Z + LANE

TM_IN = 256
TM_MIX = 512
TM_X = 512
TM_FFN = 512
TF_FFN = D_FF // 2
GLA_HG = 2


def _rms(x, w):
    return x * lax.rsqrt(jnp.mean(x * x, axis=-1, keepdims=True) + EPS) * w


def _dot(a, b):
    return jnp.dot(a, b, preferred_element_type=F32)


def _dot_nt(a, b):
    return lax.dot_general(a, b, (((1,), (1,)), ((), ())), preferred_element_type=F32)


def _dot_tn(a, b):
    return lax.dot_general(a, b, (((0,), (0,)), ((), ())), preferred_element_type=F32)


def _params(*sem):
    return pltpu.CompilerParams(dimension_semantics=sem, vmem_limit_bytes=VMEM_LIMIT)


def _inproj_kernel(x_ref, ln_ref, w_ref, qn_ref, kn_ref, cos_ref, sin_ref,
                   gqk_ref, gv_ref, gr_ref, aq_ref, ak_ref, av_ref, gate_ref, z_ref):
    h = _rms(x_ref[...], ln_ref[...]).astype(BF16)

    def proj(lo, hi):
        return _dot(h, w_ref[:, lo:hi])

    gqk_ref[...] = proj(C_GQK, C_GV)
    gv_ref[...] = proj(C_GV, C_GR).astype(BF16)
    gr_ref[...] = proj(C_GR, C_AQ)

    cos = cos_ref[...]
    sin = sin_ref[...]
    lane = lax.broadcasted_iota(jnp.int32, cos.shape, 1)
    first_half = (lane % 64) < 32

    def norm_rope(y, gain, scale):
        yn = _rms(y, gain)
        partner = jnp.where(first_half, pltpu.roll(yn, LANE - 32, 1), pltpu.roll(yn, 32, 1))
        return ((yn * cos + partner * sin) * scale).astype(BF16)

    aq = proj(C_AQ, C_AK)
    for j in range(ATT_HEADS):
        sl = slice(j * ATT_HD, (j + 1) * ATT_HD)
        aq_ref[:, sl] = norm_rope(aq[:, sl], qn_ref[...], ATT_HD ** -0.5)
    ak = proj(C_AK, C_AV)
    for j in range(ATT_KV_HEADS):
        sl = slice(j * ATT_HD, (j + 1) * ATT_HD)
        ak_ref[:, sl] = norm_rope(ak[:, sl], kn_ref[...], 1.0)
    av_ref[...] = proj(C_AV, C_GATE).astype(BF16)
    gate_ref[...] = proj(C_GATE, C_Z)
    z_ref[...] = proj(C_Z, C_END)


def _inproj(x, ln, w, qn, kn, cos, sin, seq):
    n = x.shape[0]
    tm = TM_IN
    tiles_per_seq = seq // tm
    row = lambda i: (i, 0)
    const = lambda i: (0, 0)
    pos = lambda i: (i % tiles_per_seq, 0)
    outs = [
        (2 * GLA_QK, F32), (GLA_V, BF16), (GLA_V, F32), (ATT_Q, BF16),
        (ATT_KV, BF16), (ATT_KV, BF16), (2 * D_MODEL, F32), (LANE, F32),
    ]
    return pl.pallas_call(
        _inproj_kernel,
        grid=(n // tm,),
        in_specs=[
            pl.BlockSpec((tm, D_MODEL), row),
            pl.BlockSpec((1, D_MODEL), const),
            pl.BlockSpec((D_MODEL, C_END), const, pipeline_mode=pl.Buffered(1)),
            pl.BlockSpec((1, ATT_HD), const),
            pl.BlockSpec((1, ATT_HD), const),
            pl.BlockSpec((tm, ATT_HD), pos),
            pl.BlockSpec((tm, ATT_HD), pos),
        ],
        out_specs=[pl.BlockSpec((tm, w_), row) for w_, _ in outs],
        out_shape=[jax.ShapeDtypeStruct((n, w_), dt) for w_, dt in outs],
        compiler_params=_params("parallel"),
        name="inproj",
    )(x, ln, w, qn, kn, cos, sin)


def _log_sigmoid(u):
    return jnp.minimum(u, 0.0) - jnp.log1p(jnp.exp(-jnp.abs(u)))


def _gla_kernel(q_ref, k_ref, v_ref, r_ref, z_ref, wf_ref, wb_ref, bf_ref, bb_ref, gn_ref,
                o_ref, gf_s, gb_s, st_s, oacc_s):
    seq = q_ref.shape[0]
    n_chunks = seq // GLA_CHUNK
    c = GLA_CHUNK

    z = z_ref[...].astype(BF16)
    gf_s[...] = _log_sigmoid(_dot(z, wf_ref[...]) + bf_ref[...]) * (1.0 / GLA_TAU)
    gb_s[...] = _log_sigmoid(_dot(z, wb_ref[...]) + bb_ref[...]) * (1.0 / GLA_TAU)
    st_s[...] = jnp.zeros_like(st_s)
    oacc_s[...] = jnp.zeros_like(oacc_s)

    row = lax.broadcasted_iota(jnp.int32, (c, c), 0)
    col = lax.broadcasted_iota(jnp.int32, (c, c), 1)
    tri = (jnp.where(col <= row, 1.0, 0.0).astype(BF16), jnp.where(col >= row, 1.0, 0.0).astype(BF16))
    keep = (col <= row, col > row)
    g_s = (gf_s, gb_s)

    def step(n, carry):
        for d in range(2):
            chunk = n if d == 0 else n_chunks - 1 - n
            r0 = pl.multiple_of(chunk * c, c)
            rows = pl.ds(r0, c)
            for h in range(GLA_HG):
                ksl = slice(h * GLA_DK, (h + 1) * GLA_DK)
                vsl = slice(h * GLA_DV, (h + 1) * GLA_DV)
                g = g_s[d][rows, ksl]
                g_hi = g.astype(BF16)
                g_lo = (g - g_hi.astype(F32)).astype(BF16)
                b = _dot(tri[d], g_hi) + _dot(tri[d], g_lo)
                b_tot = b[c - 1:c, :] if d == 0 else b[0:1, :]
                q = q_ref[rows, ksl] * (GLA_DK ** -0.5)
                k = k_ref[rows, ksl]
                qg = (q * jnp.exp(b)).astype(BF16)
                kg = (k * jnp.exp(-b)).astype(BF16)
                k_end = (k * jnp.exp(b_tot - b)).astype(BF16)
                a = jnp.where(keep[d], _dot_nt(qg, kg), 0.0).astype(BF16)
                v = v_ref[rows, vsl]
                st = st_s[d * GLA_HG + h]
                o = _dot(a, v) + _dot_nt(qg, st.astype(BF16))
                oacc_s[rows, vsl] += o
                st_s[d * GLA_HG + h] = jnp.exp(b_tot) * st + _dot_tn(v, k_end)
        return carry

    lax.fori_loop(0, n_chunks, step, 0)

    for h in range(GLA_HG):
        vsl = slice(h * GLA_DV, (h + 1) * GLA_DV)
        o = _rms(oacc_s[:, vsl], gn_ref[...])
        r = r_ref[:, vsl]
        o_ref[:, vsl] = (o * (r * jax.nn.sigmoid(r))).astype(BF16)


def _gla(gqk, gv, gr, z, wdec, bdec, gnorm, seq):
    n = gqk.shape[0]
    nb = n // seq
    hg = GLA_HG
    groups = GLA_HEADS // hg
    return pl.pallas_call(
        _gla_kernel,
        grid=(nb, groups),
        in_specs=[
            pl.BlockSpec((seq, hg * GLA_DK), lambda b, g: (b, g)),
            pl.BlockSpec((seq, hg * GLA_DK), lambda b, g: (b, groups + g)),
            pl.BlockSpec((seq, hg * GLA_DV), lambda b, g: (b, g)),
            pl.BlockSpec((seq, hg * GLA_DV), lambda b, g: (b, g)),
            pl.BlockSpec((seq, LANE), lambda b, g: (b, 0)),
            pl.BlockSpec((LANE, hg * GLA_DK), lambda b, g: (0, g)),
            pl.BlockSpec((LANE, hg * GLA_DK), lambda b, g: (0, groups + g)),
            pl.BlockSpec((1, hg * GLA_DK), lambda b, g: (0, g)),
            pl.BlockSpec((1, hg * GLA_DK), lambda b, g: (0, groups + g)),
            pl.BlockSpec((1, GLA_DV), lambda b, g: (0, 0)),
        ],
        out_specs=pl.BlockSpec((seq, hg * GLA_DV), lambda b, g: (b, g)),
        out_shape=jax.ShapeDtypeStruct((n, GLA_V), BF16),
        scratch_shapes=[
            pltpu.VMEM((seq, hg * GLA_DK), F32),
            pltpu.VMEM((seq, hg * GLA_DK), F32),
            pltpu.VMEM((2 * hg, GLA_DV, GLA_DK), F32),
            pltpu.VMEM((seq, hg * GLA_DV), F32),
        ],
        compiler_params=_params("parallel", "parallel"),
        name="gla",
    )(gqk, gqk, gv, gr, z, wdec, wdec, bdec, bdec, gnorm)


def _attn_kernel(q_ref, k_ref, v_ref, o_ref):
    hd = ATT_HD
    tq = q_ref.shape[0]
    q = jnp.concatenate([q_ref[:, g * hd:(g + 1) * hd] for g in range(ATT_GROUP)], axis=0)
    s = _dot_nt(q, k_ref[...])
    p = jnp.exp(s - jnp.max(s, axis=-1, keepdims=True))
    l = jnp.sum(p, axis=-1, keepdims=True)
    o = _dot(p.astype(BF16), v_ref[...]) / l
    for g in range(ATT_GROUP):
        o_ref[:, g * hd:(g + 1) * hd] = o[g * tq:(g + 1) * tq].astype(BF16)


def _attn(aq, ak, av, seq):
    n = aq.shape[0]
    nb = n // seq
    nq = seq // Q_BLOCK
    qw = ATT_GROUP * ATT_HD
    return pl.pallas_call(
        _attn_kernel,
        grid=(nb, ATT_KV_HEADS, nq),
        in_specs=[
            pl.BlockSpec((Q_BLOCK, qw), lambda b, j, i: (b * nq + i, j)),
            pl.BlockSpec((seq, ATT_HD), lambda b, j, i: (b, j)),
            pl.BlockSpec((seq, ATT_HD), lambda b, j, i: (b, j)),
        ],
        out_specs=pl.BlockSpec((Q_BLOCK, qw), lambda b, j, i: (b * nq + i, j)),
        out_shape=jax.ShapeDtypeStruct((n, ATT_Q), BF16),
        compiler_params=_params("parallel", "parallel", "arbitrary"),
        name="attn",
    )(aq, ak, av)


def _mix_kernel(x_ref, og_ref, oa_ref, gate_ref, wbg_ref, wba_ref, wout_ref, lnpost_ref, lnx_ref,
                wq_ref, x1_ref, qx_ref):
    out_a = _dot(og_ref[...], wbg_ref[...])
    out_b = _dot(oa_ref[...], wba_ref[...])
    mixed = (jax.nn.sigmoid(gate_ref[:, :D_MODEL]) * out_a
             + jax.nn.sigmoid(gate_ref[:, D_MODEL:]) * out_b)
    y = _dot(mixed.astype(BF16), wout_ref[...])
    x1 = x_ref[...] + _rms(y, lnpost_ref[...])
    x1_ref[...] = x1
    h = _rms(x1, lnx_ref[...]).astype(BF16)
    qx_ref[...] = (_dot(h, wq_ref[...]) * (X_HD ** -0.5)).astype(BF16)


def _mix(x, og, oa, gate, wbg, wba, wout, lnpost, lnx, wq):
    n = x.shape[0]
    tm = TM_MIX
    row = lambda i: (i, 0)
    const = lambda i: (0, 0)
    sq = pl.BlockSpec((D_MODEL, D_MODEL), const)
    vec = pl.BlockSpec((1, D_MODEL), const)
    return pl.pallas_call(
        _mix_kernel,
        grid=(n // tm,),
        in_specs=[
            pl.BlockSpec((tm, D_MODEL), row), pl.BlockSpec((tm, GLA_V), row),
            pl.BlockSpec((tm, ATT_Q), row), pl.BlockSpec((tm, 2 * D_MODEL), row),
            sq, sq, sq, vec, vec, sq,
        ],
        out_specs=[pl.BlockSpec((tm, D_MODEL), row), pl.BlockSpec((tm, D_MODEL), row)],
        out_shape=[jax.ShapeDtypeStruct((n, D_MODEL), F32), jax.ShapeDtypeStruct((n, D_MODEL), BF16)],
        compiler_params=_params("parallel"),
        name="mix",
    )(x, og, oa, gate, wbg, wba, wout, lnpost, lnx, wq)


def _memkv_kernel(m_ref, ln_ref, w_ref, kv_ref):
    m = _rms(m_ref[...], ln_ref[...]).astype(BF16)
    kv_ref[...] = _dot(m, w_ref[...]).astype(BF16)


def _memkv(mem, ln, wkv):
    n = mem.shape[0]
    tm = 256
    return pl.pallas_call(
        _memkv_kernel,
        grid=(n // tm,),
        in_specs=[
            pl.BlockSpec((tm, D_MODEL), lambda i: (i, 0)),
            pl.BlockSpec((1, D_MODEL), lambda i: (0, 0)),
            pl.BlockSpec((D_MODEL, 2 * D_MODEL), lambda i: (0, 0)),
        ],
        out_specs=pl.BlockSpec((tm, 2 * D_MODEL), lambda i: (i, 0)),
        out_shape=jax.ShapeDtypeStruct((n, 2 * D_MODEL), BF16),
        compiler_params=_params("parallel"),
        name="memkv",
    )(mem, ln, wkv)


def _xattn_kernel(x1_ref, qx_ref, kv_ref, wo_ref, lnpost_ref, lnffn_ref, x2_ref, h_ref):
    heads = []
    for j in range(X_HEADS):
        ksl = slice(j * X_HD, (j + 1) * X_HD)
        vsl = slice(D_MODEL + j * X_HD, D_MODEL + (j + 1) * X_HD)
        s = _dot_nt(qx_ref[:, ksl], kv_ref[:, ksl])
        p = jnp.exp(s - jnp.max(s, axis=-1, keepdims=True))
        l = jnp.sum(p, axis=-1, keepdims=True)
        heads.append((_dot(p.astype(BF16), kv_ref[:, vsl]) / l).astype(BF16))
    o = jnp.concatenate(heads, axis=1)
    x2 = x1_ref[...] + _rms(_dot(o, wo_ref[...]), lnpost_ref[...])
    x2_ref[...] = x2
    h_ref[...] = _rms(x2, lnffn_ref[...]).astype(BF16)


def _xattn(x1, qx, kv, wo, lnpost, lnffn, seq, n_mem):
    n = x1.shape[0]
    tm = TM_X
    tiles_per_seq = seq // tm
    row = lambda i: (i, 0)
    const = lambda i: (0, 0)
    return pl.pallas_call(
        _xattn_kernel,
        grid=(n // tm,),
        in_specs=[
            pl.BlockSpec((tm, D_MODEL), row), pl.BlockSpec((tm, D_MODEL), row),
            pl.BlockSpec((n_mem, 2 * D_MODEL), lambda i: (i // tiles_per_seq, 0)),
            pl.BlockSpec((D_MODEL, D_MODEL), const),
            pl.BlockSpec((1, D_MODEL), const), pl.BlockSpec((1, D_MODEL), const),
        ],
        out_specs=[pl.BlockSpec((tm, D_MODEL), row), pl.BlockSpec((tm, D_MODEL), row)],
        out_shape=[jax.ShapeDtypeStruct((n, D_MODEL), F32), jax.ShapeDtypeStruct((n, D_MODEL), BF16)],
        compiler_params=_params("parallel"),
        name="xattn",
    )(x1, qx, kv, wo, lnpost, lnffn)


def _ffn_kernel(h_ref, x2_ref, wg_ref, wu_ref, wo_ref, ln_ref, y_ref, acc_ref):
    f = pl.program_id(1)
    h = h_ref[...]
    g = _dot(h, wg_ref[...])
    u = _dot(h, wu_ref[...])
    part = _dot((g * jax.nn.sigmoid(g) * u).astype(BF16), wo_ref[...])

    @pl.when(f == 0)
    def _():
        acc_ref[...] = part

    @pl.when(f > 0)
    def _():
        acc_ref[...] += part

    @pl.when(f == pl.num_programs(1) - 1)
    def _():
        y_ref[...] = x2_ref[...] + _rms(acc_ref[...], ln_ref[...])


def _ffn(h, x2, wi, wo, ln):
    n = h.shape[0]
    tm = TM_FFN
    tf = TF_FFN
    nf = D_FF // tf
    return pl.pallas_call(
        _ffn_kernel,
        grid=(n // tm, nf),
        in_specs=[
            pl.BlockSpec((tm, D_MODEL), lambda i, f: (i, 0)),
            pl.BlockSpec((tm, D_MODEL), lambda i, f: (i, 0)),
            pl.BlockSpec((D_MODEL, tf), lambda i, f: (0, f)),
            pl.BlockSpec((D_MODEL, tf), lambda i, f: (0, nf + f)),
            pl.BlockSpec((tf, D_MODEL), lambda i, f: (f, 0)),
            pl.BlockSpec((1, D_MODEL), lambda i, f: (0, 0)),
        ],
        out_specs=pl.BlockSpec((tm, D_MODEL), lambda i, f: (i, 0)),
        out_shape=jax.ShapeDtypeStruct((n, D_MODEL), F32),
        scratch_shapes=[pltpu.VMEM((tm, D_MODEL), F32)],
        compiler_params=_params("parallel", "arbitrary"),
        name="ffn",
    )(h, x2, wi, wi, wo, ln)


def _rope_tables(seq):
    rows = seq // GRID_W
    row = jnp.broadcast_to(jnp.arange(rows)[:, None], (rows, GRID_W)).reshape(-1).astype(F32)
    col = jnp.broadcast_to(jnp.arange(GRID_W)[None, :], (rows, GRID_W)).reshape(-1).astype(F32)
    half = ATT_HD // 2
    inv_freq = ROPE_THETA ** (-jnp.arange(0, half, 2, dtype=F32) / half)
    ang_r = row[:, None] * inv_freq[None, :]
    ang_c = col[:, None] * inv_freq[None, :]
    cr, sr, cc, sc = jnp.cos(ang_r), jnp.sin(ang_r), jnp.cos(ang_c), jnp.sin(ang_c)
    return (jnp.concatenate([cr, cr, cc, cc], axis=-1),
            jnp.concatenate([-sr, sr, -sc, sc], axis=-1))


def _pack_w_in(w_in):
    o = [0]
    for s in (GLA_QK, GLA_QK, GLA_V, GLA_V, GLA_RANK, GLA_RANK, ATT_Q, ATT_KV, ATT_KV, D_MODEL, D_MODEL):
        o.append(o[-1] + s)
    gq, gk, gv, gr, zf, zb, aq, ak, av, ga, gb = [w_in[:, o[i]:o[i + 1]] for i in range(11)]
    pad = jnp.zeros((D_MODEL, LANE - 2 * GLA_RANK), w_in.dtype)
    return jnp.concatenate([gq, gk, gv, gr, aq, ak, av, ga, gb, zf, zb, pad], axis=1).astype(BF16)


def _pack_decay(wa_f, ba_f, wa_b, ba_b):
    w = jnp.zeros((LANE, 2 * GLA_QK), F32)
    w = w.at[:GLA_RANK, :GLA_QK].set(wa_f).at[GLA_RANK:2 * GLA_RANK, GLA_QK:].set(wa_b)
    return w.astype(BF16), jnp.concatenate([ba_f, ba_b])[None, :]


def _layer(x, mem, seq, n_mem, ln_mix_pre, w_in, gla_wa_f, gla_ba_f, gla_wa_b, gla_ba_b, gla_norm,
           att_q_norm, att_k_norm, w_branch_gla, w_branch_att, w_out, ln_mix_post,
           ln_x_pre, ln_mem, x_wq, x_wkv, x_wo, ln_x_post, ln_ffn_pre, ffn_wi, ffn_wo, ln_ffn_post):
    vec = lambda v: v[None, :]
    cos, sin = _rope_tables(seq)
    wdec, bdec = _pack_decay(gla_wa_f, gla_ba_f, gla_wa_b, gla_ba_b)
    gqk, gv, gr, aq, ak, av, gate, z = _inproj(
        x, vec(ln_mix_pre), _pack_w_in(w_in), vec(att_q_norm), vec(att_k_norm), cos, sin, seq)
    og = _gla(gqk, gv, gr, z, wdec, bdec, vec(gla_norm), seq)
    oa = _attn(aq, ak, av, seq)
    x1, qx = _mix(x, og, oa, gate, w_branch_gla.astype(BF16), w_branch_att.astype(BF16),
                  w_out.astype(BF16), vec(ln_mix_post), vec(ln_x_pre), x_wq.astype(BF16))
    kv = _memkv(mem, vec(ln_mem), x_wkv.astype(BF16))
    x2, h = _xattn(x1, qx, kv, x_wo.astype(BF16), vec(ln_x_post), vec(ln_ffn_pre), seq, n_mem)
    return _ffn(h, x2, ffn_wi.astype(BF16), ffn_wo.astype(BF16), vec(ln_ffn_post))


def kernel(x_prompt, x_sample, mem_prompt, mem_sample, ln_mix_pre, w_in, gla_wa_f, gla_ba_f, gla_wa_b, gla_ba_b, gla_norm, att_q_norm, att_k_norm, w_branch_gla, w_branch_att, w_out, ln_mix_post, ln_x_pre, ln_mem, x_wq, x_wkv, x_wo, ln_x_post, ln_ffn_pre, ffn_wi, ffn_wo, ln_ffn_post):
    params = (ln_mix_pre, w_in, gla_wa_f, gla_ba_f, gla_wa_b, gla_ba_b, gla_norm,
              att_q_norm, att_k_norm, w_branch_gla, w_branch_att, w_out, ln_mix_post,
              ln_x_pre, ln_mem, x_wq, x_wkv, x_wo, ln_x_post,
              ln_ffn_pre, ffn_wi, ffn_wo, ln_ffn_post)
    bp, seq, d = x_prompt.shape
    bs = x_sample.shape[0]
    n_mem = mem_prompt.shape[1]
    assert x_sample.shape[1:] == (seq, d) and mem_sample.shape[1:] == (n_mem, d) and d == D_MODEL
    x = jnp.concatenate([x_prompt, x_sample], axis=0).reshape((bp + bs) * seq, d)
    mem = jnp.concatenate([mem_prompt, mem_sample], axis=0).reshape((bp + bs) * n_mem, d)
    for l in range(ln_mix_pre.shape[0]):
        x = _layer(x, mem, seq, n_mem, *[p[l] for p in params])
    y = x.reshape(bp + bs, seq, d)
    return (y[:bp], y[bp:])
```

```python
import functools

import jax
import jax.numpy as jnp
from jax import lax
from jax.experimental import pallas as pl
from jax.experimental.pallas import tpu as pltpu

F32 = jnp.float32
BF16 = jnp.bfloat16

D_MODEL = 1024
GRID_W = 64
EPS = 1e-6
GLA_HEADS = 4
GLA_DK = 128
GLA_DV = 256
GLA_QK = GLA_HEADS * GLA_DK
GLA_V = GLA_HEADS * GLA_DV
GLA_RANK = 16
GLA_TAU = 16.0
GLA_CHUNK = 64
ATT_HEADS = 8
ATT_KV_HEADS = 2
ATT_GROUP = ATT_HEADS // ATT_KV_HEADS
ATT_HD = 128
ATT_Q = ATT_HEADS * ATT_HD
ATT_KV = ATT_KV_HEADS * ATT_HD
ROPE_THETA = 10000.0
Q_BLOCK = 128
X_HEADS = 4
X_HD = D_MODEL // X_HEADS
D_FF = 2816

LANE = 128
VMEM_LIMIT = 56 * 1024 * 1024

C_GQK = 0
C_GV = C_GQK + 2 * GLA_QK
C_GR = C_GV + GLA_V
C_AQ = C_GR + GLA_V
C_AK = C_AQ + ATT_Q
C_AV = C_AK + ATT_KV
C_GATE = C_AV + ATT_KV
C_Z = C_GATE + 2 * D_MODEL
C_END = C_Z + LANE

TM_IN = 256
TM_MIX = 512
TM_X = 512
TM_FFN = 512
TF_FFN = D_FF // 2
GLA_HG = 2
GLA_BLOCK = 8


def _rms(x, w):
    return x * lax.rsqrt(jnp.mean(x * x, axis=-1, keepdims=True) + EPS) * w


def _dot(a, b):
    return jnp.dot(a, b, preferred_element_type=F32)


def _dot_nt(a, b):
    return lax.dot_general(a, b, (((1,), (1,)), ((), ())), preferred_element_type=F32)


def _dot_tn(a, b):
    return lax.dot_general(a, b, (((0,), (0,)), ((), ())), preferred_element_type=F32)


def _params(*sem):
    return pltpu.CompilerParams(dimension_semantics=sem, vmem_limit_bytes=VMEM_LIMIT)


def _inproj_kernel(x_ref, ln_ref, w_ref, qn_ref, kn_ref, cos_ref, sin_ref,
                   gqk_ref, gv_ref, gr_ref, aq_ref, ak_ref, av_ref, gate_ref, z_ref):
    h = _rms(x_ref[...], ln_ref[...]).astype(BF16)

    def proj(lo, hi):
        return _dot(h, w_ref[:, lo:hi])

    gqk_ref[...] = proj(C_GQK, C_GV)
    gv_ref[...] = proj(C_GV, C_GR).astype(BF16)
    gr_ref[...] = proj(C_GR, C_AQ)

    cos = cos_ref[...]
    sin = sin_ref[...]
    lane = lax.broadcasted_iota(jnp.int32, cos.shape, 1)
    first_half = (lane % 64) < 32

    def norm_rope(y, gain, scale):
        yn = _rms(y, gain)
        partner = jnp.where(first_half, pltpu.roll(yn, LANE - 32, 1), pltpu.roll(yn, 32, 1))
        return ((yn * cos + partner * sin) * scale).astype(BF16)

    aq = proj(C_AQ, C_AK)
    for j in range(ATT_HEADS):
        sl = slice(j * ATT_HD, (j + 1) * ATT_HD)
        aq_ref[:, sl] = norm_rope(aq[:, sl], qn_ref[...], ATT_HD ** -0.5)
    ak = proj(C_AK, C_AV)
    for j in range(ATT_KV_HEADS):
        sl = slice(j * ATT_HD, (j + 1) * ATT_HD)
        ak_ref[:, sl] = norm_rope(ak[:, sl], kn_ref[...], 1.0)
    av_ref[...] = proj(C_AV, C_GATE).astype(BF16)
    gate_ref[...] = proj(C_GATE, C_Z)
    z_ref[...] = proj(C_Z, C_END)


def _inproj(x, ln, w, qn, kn, cos, sin, seq):
    n = x.shape[0]
    tm = TM_IN
    tiles_per_seq = seq // tm
    row = lambda i: (i, 0)
    const = lambda i: (0, 0)
    pos = lambda i: (i % tiles_per_seq, 0)
    outs = [
        (2 * GLA_QK, F32), (GLA_V, BF16), (GLA_V, F32), (ATT_Q, BF16),
        (ATT_KV, BF16), (ATT_KV, BF16), (2 * D_MODEL, F32), (LANE, F32),
    ]
    return pl.pallas_call(
        _inproj_kernel,
        grid=(n // tm,),
        in_specs=[
            pl.BlockSpec((tm, D_MODEL), row),
            pl.BlockSpec((1, D_MODEL), const),
            pl.BlockSpec((D_MODEL, C_END), const, pipeline_mode=pl.Buffered(1)),
            pl.BlockSpec((1, ATT_HD), const),
            pl.BlockSpec((1, ATT_HD), const),
            pl.BlockSpec((tm, ATT_HD), pos),
            pl.BlockSpec((tm, ATT_HD), pos),
        ],
        out_specs=[pl.BlockSpec((tm, w_), row) for w_, _ in outs],
        out_shape=[jax.ShapeDtypeStruct((n, w_), dt) for w_, dt in outs],
        compiler_params=_params("parallel"),
        name="inproj",
    )(x, ln, w, qn, kn, cos, sin)


def _log_sigmoid(u):
    return jnp.minimum(u, 0.0) - jnp.log(1.0 + jnp.exp(-jnp.abs(u)))


def _gla_chain(d, g, q, k, v, st, tri, keep):
    c = GLA_CHUNK
    nc = g.shape[0] // c
    chunk = lambda t, i: t[i * c:(i + 1) * c]
    g_hi = g.astype(BF16)
    g_lo = (g - g_hi.astype(F32)).astype(BF16)
    g_hl = jnp.concatenate([g_hi, g_lo], axis=1)
    b_parts, tot_parts, tot_rows = [], [], []
    for i in range(nc):
        b2 = _dot(tri, chunk(g_hl, i))
        b_i = b2[:, :GLA_DK] + b2[:, GLA_DK:]
        t_i = b_i[c - 1:c] if d == 0 else b_i[0:1]
        b_parts.append(b_i)
        tot_rows.append(t_i)
        tot_parts.append(jnp.broadcast_to(t_i, (c, GLA_DK)))
    yield
    b = jnp.concatenate(b_parts, axis=0)
    b_tot = jnp.concatenate(tot_parts, axis=0)
    qg = (q * jnp.exp(b)).astype(BF16)
    kg = (k * jnp.exp(-b)).astype(BF16)
    k_end = (k * jnp.exp(b_tot - b)).astype(BF16)
    yield
    a, u, dec = [], [], []
    for i in range(nc):
        a.append(jnp.where(keep, _dot_nt(chunk(qg, i), chunk(kg, i)), 0.0).astype(BF16))
        u.append(_dot_tn(chunk(k_end, i), chunk(v, i)))
        dec_col = jnp.broadcast_to(jnp.exp(tot_rows[i]), (GLA_DK, GLA_DK)).T
        dec.append(jnp.concatenate([dec_col] * (GLA_DV // GLA_DK), axis=1))
    yield
    st_in = [None] * nc
    for i in (range(nc) if d == 0 else range(nc - 1, -1, -1)):
        st_in[i] = st.astype(BF16)
        st = dec[i] * st + u[i]
    yield
    o = [_dot(jnp.concatenate([chunk(qg, i), a[i]], axis=1),
              jnp.concatenate([st_in[i], chunk(v, i)], axis=0)) for i in range(nc)]
    return jnp.concatenate(o, axis=0), st


def _run_interleaved(chains):
    results = [None] * len(chains)
    live = list(enumerate(chains))
    while live:
        still = []
        for idx, gen in live:
            try:
                next(gen)
                still.append((idx, gen))
            except StopIteration as stop:
                results[idx] = stop.value
        live = still
    return results


def _gla_kernel(q_ref, k_ref, v_ref, r_ref, z_ref, wf_ref, wb_ref, bf_ref, bb_ref, gn_ref,
                o_ref, gf_s, gb_s, st_s, oacc_s):
    seq = q_ref.shape[0]
    c = GLA_CHUNK
    br = GLA_BLOCK * c
    n_blocks = seq // br

    z = z_ref[...].astype(BF16)
    gf_s[...] = _log_sigmoid(_dot(z, wf_ref[...]) + bf_ref[...]) * (1.0 / GLA_TAU)
    gb_s[...] = _log_sigmoid(_dot(z, wb_ref[...]) + bb_ref[...]) * (1.0 / GLA_TAU)
    st_s[...] = jnp.zeros_like(st_s)
    oacc_s[...] = jnp.zeros_like(oacc_s)

    row = lax.broadcasted_iota(jnp.int32, (c, c), 0)
    col = lax.broadcasted_iota(jnp.int32, (c, c), 1)
    tri = (jnp.where(col <= row, 1.0, 0.0).astype(BF16), jnp.where(col >= row, 1.0, 0.0).astype(BF16))
    keep = (col <= row, col > row)
    g_s = (gf_s, gb_s)

    def step(n, carry):
        where, chains = [], []
        for d in range(2):
            block = n if d == 0 else n_blocks - 1 - n
            rows = pl.ds(pl.multiple_of(block * br, br), br)
            for h in range(GLA_HG):
                ksl = slice(h * GLA_DK, (h + 1) * GLA_DK)
                vsl = slice(h * GLA_DV, (h + 1) * GLA_DV)
                where.append((rows, vsl, d * GLA_HG + h))
                chains.append(_gla_chain(d, g_s[d][rows, ksl], q_ref[rows, ksl] * (GLA_DK ** -0.5),
                                         k_ref[rows, ksl], v_ref[rows, vsl], st_s[d * GLA_HG + h],
                                         tri[d], keep[d]))
        for (rows, vsl, slot), (o, st) in zip(where, _run_interleaved(chains)):
            oacc_s[rows, vsl] += o
            st_s[slot] = st
        return carry

    lax.fori_loop(0, n_blocks, step, 0)

    for h in range(GLA_HG):
        vsl = slice(h * GLA_DV, (h + 1) * GLA_DV)
        o = _rms(oacc_s[:, vsl], gn_ref[...])
        r = r_ref[:, vsl]
        o_ref[:, vsl] = (o * (r * jax.nn.sigmoid(r))).astype(BF16)


def _gla(gqk, gv, gr, z, wdec, bdec, gnorm, seq):
    n = gqk.shape[0]
    nb = n // seq
    hg = GLA_HG
    groups = GLA_HEADS // hg
    return pl.pallas_call(
        _gla_kernel,
        grid=(nb, groups),
        in_specs=[
            pl.BlockSpec((seq, hg * GLA_DK), lambda b, g: (b, g)),
            pl.BlockSpec((seq, hg * GLA_DK), lambda b, g: (b, groups + g)),
            pl.BlockSpec((seq, hg * GLA_DV), lambda b, g: (b, g)),
            pl.BlockSpec((seq, hg * GLA_DV), lambda b, g: (b, g)),
            pl.BlockSpec((seq, LANE), lambda b, g: (b, 0)),
            pl.BlockSpec((LANE, hg * GLA_DK), lambda b, g: (0, g)),
            pl.BlockSpec((LANE, hg * GLA_DK), lambda b, g: (0, groups + g)),
            pl.BlockSpec((1, hg * GLA_DK), lambda b, g: (0, g)),
            pl.BlockSpec((1, hg * GLA_DK), lambda b, g: (0, groups + g)),
            pl.BlockSpec((1, GLA_DV), lambda b, g: (0, 0)),
        ],
        out_specs=pl.BlockSpec((seq, hg * GLA_DV), lambda b, g: (b, g)),
        out_shape=jax.ShapeDtypeStruct((n, GLA_V), BF16),
        scratch_shapes=[
            pltpu.VMEM((seq, hg * GLA_DK), F32),
            pltpu.VMEM((seq, hg * GLA_DK), F32),
            pltpu.VMEM((2 * hg, GLA_DK, GLA_DV), F32),
            pltpu.VMEM((seq, hg * GLA_DV), F32),
        ],
        compiler_params=_params("parallel", "parallel"),
        name="gla",
    )(gqk, gqk, gv, gr, z, wdec, wdec, bdec, bdec, gnorm)


def _attn_kernel(q_ref, k_ref, v_ref, o_ref, s_s, m_s, p_s, l_s):
    hd = ATT_HD
    tq = Q_BLOCK
    nq = q_ref.shape[0] // tq

    def rows_of(i):
        return pl.ds(pl.multiple_of(i * tq, tq), tq)

    def scores(i, slot):
        rows = rows_of(i)
        q = jnp.concatenate([q_ref[rows, g * hd:(g + 1) * hd] for g in range(ATT_GROUP)], axis=0)
        s = _dot_nt(q, k_ref[...])
        m_s[slot] = jnp.max(s, axis=-1, keepdims=True)
        s_s[slot] = s

    def softmax(slot):
        p = jnp.exp(s_s[slot] - m_s[slot])
        l_s[slot] = jnp.sum(p, axis=-1, keepdims=True)
        p_s[slot] = p.astype(BF16)

    def values(i, slot):
        rows = rows_of(i)
        o = _dot(p_s[slot], v_ref[...]) / l_s[slot]
        for g in range(ATT_GROUP):
            o_ref[rows, g * hd:(g + 1) * hd] = o[g * tq:(g + 1) * tq].astype(BF16)

    scores(0, 0)
    softmax(0)
    scores(1, 1)

    def pair(j, carry):
        values(2 * j, 0)
        softmax(1)
        scores(2 * j + 2, 0)
        values(2 * j + 1, 1)
        softmax(0)
        scores(2 * j + 3, 1)
        return carry

    lax.fori_loop(0, (nq - 2) // 2, pair, 0)
    values(nq - 2, 0)
    softmax(1)
    values(nq - 1, 1)


def _attn(aq, ak, av, seq):
    n = aq.shape[0]
    nb = n // seq
    assert (seq // Q_BLOCK) % 2 == 0
    qw = ATT_GROUP * ATT_HD
    return pl.pallas_call(
        _attn_kernel,
        grid=(nb, ATT_KV_HEADS),
        in_specs=[
            pl.BlockSpec((seq, qw), lambda b, j: (b, j)),
            pl.BlockSpec((seq, ATT_HD), lambda b, j: (b, j)),
            pl.BlockSpec((seq, ATT_HD), lambda b, j: (b, j)),
        ],
        out_specs=pl.BlockSpec((seq, qw), lambda b, j: (b, j)),
        out_shape=jax.ShapeDtypeStruct((n, ATT_Q), BF16),
        scratch_shapes=[
            pltpu.VMEM((2, ATT_GROUP * Q_BLOCK, seq), F32),
            pltpu.VMEM((2, ATT_GROUP * Q_BLOCK, 1), F32),
            pltpu.VMEM((2, ATT_GROUP * Q_BLOCK, seq), BF16),
            pltpu.VMEM((2, ATT_GROUP * Q_BLOCK, 1), F32),
        ],
        compiler_params=_params("parallel", "parallel"),
        name="attn",
    )(aq, ak, av)


def _mix_kernel(x_ref, og_ref, oa_ref, gate_ref, wbg_ref, wba_ref, wout_ref, lnpost_ref, lnx_ref,
                wq_ref, x1_ref, qx_ref):
    out_a = _dot(og_ref[...], wbg_ref[...])
    out_b = _dot(oa_ref[...], wba_ref[...])
    mixed = (jax.nn.sigmoid(gate_ref[:, :D_MODEL]) * out_a
             + jax.nn.sigmoid(gate_ref[:, D_MODEL:]) * out_b)
    y = _dot(mixed.astype(BF16), wout_ref[...])
    x1 = x_ref[...] + _rms(y, lnpost_ref[...])
    x1_ref[...] = x1
    h = _rms(x1, lnx_ref[...]).astype(BF16)
    qx_ref[...] = (_dot(h, wq_ref[...]) * (X_HD ** -0.5)).astype(BF16)


def _mix(x, og, oa, gate, wbg, wba, wout, lnpost, lnx, wq):
    n = x.shape[0]
    tm = TM_MIX
    row = lambda i: (i, 0)
    const = lambda i: (0, 0)
    sq = pl.BlockSpec((D_MODEL, D_MODEL), const)
    vec = pl.BlockSpec((1, D_MODEL), const)
    return pl.pallas_call(
        _mix_kernel,
        grid=(n // tm,),
        in_specs=[
            pl.BlockSpec((tm, D_MODEL), row), pl.BlockSpec((tm, GLA_V), row),
            pl.BlockSpec((tm, ATT_Q), row), pl.BlockSpec((tm, 2 * D_MODEL), row),
            sq, sq, sq, vec, vec, sq,
        ],
        out_specs=[pl.BlockSpec((tm, D_MODEL), row), pl.BlockSpec((tm, D_MODEL), row)],
        out_shape=[jax.ShapeDtypeStruct((n, D_MODEL), F32), jax.ShapeDtypeStruct((n, D_MODEL), BF16)],
        compiler_params=_params("parallel"),
        name="mix",
    )(x, og, oa, gate, wbg, wba, wout, lnpost, lnx, wq)


def _memkv_kernel(m_ref, ln_ref, w_ref, kv_ref):
    m = _rms(m_ref[...], ln_ref[...]).astype(BF16)
    kv_ref[...] = _dot(m, w_ref[...]).astype(BF16)


def _memkv(mem, ln, wkv):
    n = mem.shape[0]
    tm = 256
    return pl.pallas_call(
        _memkv_kernel,
        grid=(n // tm,),
        in_specs=[
            pl.BlockSpec((tm, D_MODEL), lambda i: (i, 0)),
            pl.BlockSpec((1, D_MODEL), lambda i: (0, 0)),
            pl.BlockSpec((D_MODEL, 2 * D_MODEL), lambda i: (0, 0)),
        ],
        out_specs=pl.BlockSpec((tm, 2 * D_MODEL), lambda i: (i, 0)),
        out_shape=jax.ShapeDtypeStruct((n, 2 * D_MODEL), BF16),
        compiler_params=_params("parallel"),
        name="memkv",
    )(mem, ln, wkv)


def _xattn_kernel(x1_ref, qx_ref, kv_ref, wo_ref, lnpost_ref, lnffn_ref, x2_ref, h_ref):
    heads = []
    for j in range(X_HEADS):
        ksl = slice(j * X_HD, (j + 1) * X_HD)
        vsl = slice(D_MODEL + j * X_HD, D_MODEL + (j + 1) * X_HD)
        s = _dot_nt(qx_ref[:, ksl], kv_ref[:, ksl])
        p = jnp.exp(s - jnp.max(s, axis=-1, keepdims=True))
        l = jnp.sum(p, axis=-1, keepdims=True)
        heads.append((_dot(p.astype(BF16), kv_ref[:, vsl]) / l).astype(BF16))
    o = jnp.concatenate(heads, axis=1)
    x2 = x1_ref[...] + _rms(_dot(o, wo_ref[...]), lnpost_ref[...])
    x2_ref[...] = x2
    h_ref[...] = _rms(x2, lnffn_ref[...]).astype(BF16)


def _xattn(x1, qx, kv, wo, lnpost, lnffn, seq, n_mem):
    n = x1.shape[0]
    tm = TM_X
    tiles_per_seq = seq // tm
    row = lambda i: (i, 0)
    const = lambda i: (0, 0)
    return pl.pallas_call(
        _xattn_kernel,
        grid=(n // tm,),
        in_specs=[
            pl.BlockSpec((tm, D_MODEL), row), pl.BlockSpec((tm, D_MODEL), row),
            pl.BlockSpec((n_mem, 2 * D_MODEL), lambda i: (i // tiles_per_seq, 0)),
            pl.BlockSpec((D_MODEL, D_MODEL), const),
            pl.BlockSpec((1, D_MODEL), const), pl.BlockSpec((1, D_MODEL), const),
        ],
        out_specs=[pl.BlockSpec((tm, D_MODEL), row), pl.BlockSpec((tm, D_MODEL), row)],
        out_shape=[jax.ShapeDtypeStruct((n, D_MODEL), F32), jax.ShapeDtypeStruct((n, D_MODEL), BF16)],
        compiler_params=_params("parallel"),
        name="xattn",
    )(x1, qx, kv, wo, lnpost, lnffn)


def _ffn_kernel(h_ref, x2_ref, wg_ref, wu_ref, wo_ref, ln_ref, y_ref, acc_ref):
    f = pl.program_id(1)
    h = h_ref[...]
    g = _dot(h, wg_ref[...])
    u = _dot(h, wu_ref[...])
    part = _dot((g * jax.nn.sigmoid(g) * u).astype(BF16), wo_ref[...])

    @pl.when(f == 0)
    def _():
        acc_ref[...] = part

    @pl.when(f > 0)
    def _():
        acc_ref[...] += part

    @pl.when(f == pl.num_programs(1) - 1)
    def _():
        y_ref[...] = x2_ref[...] + _rms(acc_ref[...], ln_ref[...])


def _ffn(h, x2, wi, wo, ln):
    n = h.shape[0]
    tm = TM_FFN
    tf = TF_FFN
    nf = D_FF // tf
    return pl.pallas_call(
        _ffn_kernel,
        grid=(n // tm, nf),
        in_specs=[
            pl.BlockSpec((tm, D_MODEL), lambda i, f: (i, 0)),
            pl.BlockSpec((tm, D_MODEL), lambda i, f: (i, 0)),
            pl.BlockSpec((D_MODEL, tf), lambda i, f: (0, f)),
            pl.BlockSpec((D_MODEL, tf), lambda i, f: (0, nf + f)),
            pl.BlockSpec((tf, D_MODEL), lambda i, f: (f, 0)),
            pl.BlockSpec((1, D_MODEL), lambda i, f: (0, 0)),
        ],
        out_specs=pl.BlockSpec((tm, D_MODEL), lambda i, f: (i, 0)),
        out_shape=jax.ShapeDtypeStruct((n, D_MODEL), F32),
        scratch_shapes=[pltpu.VMEM((tm, D_MODEL), F32)],
        compiler_params=_params("parallel", "arbitrary"),
        name="ffn",
    )(h, x2, wi, wi, wo, ln)


def _rope_tables(seq):
    rows = seq // GRID_W
    row = jnp.broadcast_to(jnp.arange(rows)[:, None], (rows, GRID_W)).reshape(-1).astype(F32)
    col = jnp.broadcast_to(jnp.arange(GRID_W)[None, :], (rows, GRID_W)).reshape(-1).astype(F32)
    half = ATT_HD // 2
    inv_freq = ROPE_THETA ** (-jnp.arange(0, half, 2, dtype=F32) / half)
    ang_r = row[:, None] * inv_freq[None, :]
    ang_c = col[:, None] * inv_freq[None, :]
    cr, sr, cc, sc = jnp.cos(ang_r), jnp.sin(ang_r), jnp.cos(ang_c), jnp.sin(ang_c)
    return (jnp.concatenate([cr, cr, cc, cc], axis=-1),
            jnp.concatenate([-sr, sr, -sc, sc], axis=-1))


def _pack_w_in(w_in):
    o = [0]
    for s in (GLA_QK, GLA_QK, GLA_V, GLA_V, GLA_RANK, GLA_RANK, ATT_Q, ATT_KV, ATT_KV, D_MODEL, D_MODEL):
        o.append(o[-1] + s)
    gq, gk, gv, gr, zf, zb, aq, ak, av, ga, gb = [w_in[:, o[i]:o[i + 1]] for i in range(11)]
    pad = jnp.zeros((D_MODEL, LANE - 2 * GLA_RANK), w_in.dtype)
    return jnp.concatenate([gq, gk, gv, gr, aq, ak, av, ga, gb, zf, zb, pad], axis=1).astype(BF16)


def _pack_decay(wa_f, ba_f, wa_b, ba_b):
    w = jnp.zeros((LANE, 2 * GLA_QK), F32)
    w = w.at[:GLA_RANK, :GLA_QK].set(wa_f).at[GLA_RANK:2 * GLA_RANK, GLA_QK:].set(wa_b)
    return w.astype(BF16), jnp.concatenate([ba_f, ba_b])[None, :]


def _layer(x, mem, seq, n_mem, ln_mix_pre, w_in, gla_wa_f, gla_ba_f, gla_wa_b, gla_ba_b, gla_norm,
           att_q_norm, att_k_norm, w_branch_gla, w_branch_att, w_out, ln_mix_post,
           ln_x_pre, ln_mem, x_wq, x_wkv, x_wo, ln_x_post, ln_ffn_pre, ffn_wi, ffn_wo, ln_ffn_post):
    vec = lambda v: v[None, :]
    cos, sin = _rope_tables(seq)
    wdec, bdec = _pack_decay(gla_wa_f, gla_ba_f, gla_wa_b, gla_ba_b)
    gqk, gv, gr, aq, ak, av, gate, z = _inproj(
        x, vec(ln_mix_pre), _pack_w_in(w_in), vec(att_q_norm), vec(att_k_norm), cos, sin, seq)
    og = _gla(gqk, gv, gr, z, wdec, bdec, vec(gla_norm), seq)
    oa = _attn(aq, ak, av, seq)
    x1, qx = _mix(x, og, oa, gate, w_branch_gla.astype(BF16), w_branch_att.astype(BF16),
                  w_out.astype(BF16), vec(ln_mix_post), vec(ln_x_pre), x_wq.astype(BF16))
    kv = _memkv(mem, vec(ln_mem), x_wkv.astype(BF16))
    x2, h = _xattn(x1, qx, kv, x_wo.astype(BF16), vec(ln_x_post), vec(ln_ffn_pre), seq, n_mem)
    return _ffn(h, x2, ffn_wi.astype(BF16), ffn_wo.astype(BF16), vec(ln_ffn_post))


def kernel(x_prompt, x_sample, mem_prompt, mem_sample, ln_mix_pre, w_in, gla_wa_f, gla_ba_f, gla_wa_b, gla_ba_b, gla_norm, att_q_norm, att_k_norm, w_branch_gla, w_branch_att, w_out, ln_mix_post, ln_x_pre, ln_mem, x_wq, x_wkv, x_wo, ln_x_post, ln_ffn_pre, ffn_wi, ffn_wo, ln_ffn_post):
    params = (ln_mix_pre, w_in, gla_wa_f, gla_ba_f, gla_wa_b, gla_ba_b, gla_norm,
              att_q_norm, att_k_norm, w_branch_gla, w_branch_att, w_out, ln_mix_post,
              ln_x_pre, ln_mem, x_wq, x_wkv, x_wo, ln_x_post,
              ln_ffn_pre, ffn_wi, ffn_wo, ln_ffn_post)
    bp, seq, d = x_prompt.shape
    bs = x_sample.shape[0]
    n_mem = mem_prompt.shape[1]
    assert x_sample.shape[1:] == (seq, d) and mem_sample.shape[1:] == (n_mem, d) and d == D_MODEL
    x = jnp.concatenate([x_prompt, x_sample], axis=0).reshape((bp + bs) * seq, d)
    mem = jnp.concatenate([mem_prompt, mem_sample], axis=0).reshape((bp + bs) * n_mem, d)
    for l in range(ln_mix_pre.shape[0]):
        x = _layer(x, mem, seq, n_mem, *[p[l] for p in params])
    y = x.reshape(bp + bs, seq, d)
    return (y[:bp], y[bp:])
```

```python
import functools

import jax
import jax.numpy as jnp
from jax import lax
from jax.experimental import pallas as pl
from jax.experimental.pallas import tpu as pltpu

F32 = jnp.float32
BF16 = jnp.bfloat16

D_MODEL = 1024
GRID_W = 64
EPS = 1e-6
GLA_HEADS = 4
GLA_DK = 128
GLA_DV = 256
GLA_QK = GLA_HEADS * GLA_DK
GLA_V = GLA_HEADS * GLA_DV
GLA_RANK = 16
GLA_TAU = 16.0
GLA_CHUNK = 64
ATT_HEADS = 8
ATT_KV_HEADS = 2
ATT_GROUP = ATT_HEADS // ATT_KV_HEADS
ATT_HD = 128
ATT_Q = ATT_HEADS * ATT_HD
ATT_KV = ATT_KV_HEADS * ATT_HD
ROPE_THETA = 10000.0
Q_BLOCK = 128
X_HEADS = 4
X_HD = D_MODEL // X_HEADS
D_FF = 2816

LANE = 128
VMEM_LIMIT = 56 * 1024 * 1024

C_GQK = 0
C_GV = C_GQK + 2 * GLA_QK
C_GR = C_GV + GLA_V
C_AQ = C_GR + GLA_V
C_AK = C_AQ + ATT_Q
C_AV = C_AK + ATT_KV
C_GATE = C_AV + ATT_KV
C_Z = C_GATE + 2 * D_MODEL
C_END = C_Z + LANE

TM_IN = 256
TM_MIX = 512
TM_X = 512
GLA_HG = 2
GLA_BLOCK = 8


def _rms(x, w):
    return x * lax.rsqrt(jnp.mean(x * x, axis=-1, keepdims=True) + EPS) * w


def _dot(a, b):
    return jnp.dot(a, b, preferred_element_type=F32)


def _dot_nt(a, b):
    return lax.dot_general(a, b, (((1,), (1,)), ((), ())), preferred_element_type=F32)


def _dot_tn(a, b):
    return lax.dot_general(a, b, (((0,), (0,)), ((), ())), preferred_element_type=F32)


def _params(*sem):
    return pltpu.CompilerParams(dimension_semantics=sem, vmem_limit_bytes=VMEM_LIMIT)


def _group_specs(tm, width, tiles_a):
    return (pl.BlockSpec((tm, width), lambda i: (jnp.minimum(i, tiles_a - 1), 0)),
            pl.BlockSpec((tm, width), lambda i: (jnp.maximum(i - tiles_a, 0), 0)))


def _group_read(tiles_a, a_ref, b_ref):
    return jnp.where(pl.program_id(0) < tiles_a, a_ref[...], b_ref[...])


def _inproj_kernel(tiles_a, xa_ref, xb_ref, ln_ref, w_ref, qn_ref, kn_ref, cos_ref, sin_ref,
                   gqk_ref, gv_ref, gr_ref, aq_ref, ak_ref, av_ref, gate_ref, z_ref):
    h = _rms(_group_read(tiles_a, xa_ref, xb_ref), ln_ref[...]).astype(BF16)

    def proj(lo, hi):
        return _dot(h, w_ref[:, lo:hi])

    gqk_ref[...] = proj(C_GQK, C_GV)
    gv_ref[...] = proj(C_GV, C_GR).astype(BF16)
    gr_ref[...] = proj(C_GR, C_AQ)

    cos = cos_ref[...]
    sin = sin_ref[...]
    lane = lax.broadcasted_iota(jnp.int32, cos.shape, 1)
    first_half = (lane % 64) < 32

    def norm_rope(y, gain, scale):
        yn = _rms(y, gain)
        partner = jnp.where(first_half, pltpu.roll(yn, LANE - 32, 1), pltpu.roll(yn, 32, 1))
        return ((yn * cos + partner * sin) * scale).astype(BF16)

    aq = proj(C_AQ, C_AK)
    for j in range(ATT_HEADS):
        sl = slice(j * ATT_HD, (j + 1) * ATT_HD)
        aq_ref[:, sl] = norm_rope(aq[:, sl], qn_ref[...], ATT_HD ** -0.5)
    ak = proj(C_AK, C_AV)
    for j in range(ATT_KV_HEADS):
        sl = slice(j * ATT_HD, (j + 1) * ATT_HD)
        ak_ref[:, sl] = norm_rope(ak[:, sl], kn_ref[...], 1.0)
    av_ref[...] = proj(C_AV, C_GATE).astype(BF16)
    gate_ref[...] = proj(C_GATE, C_Z)
    z_ref[...] = proj(C_Z, C_END)


def _inproj(xa, xb, ln, w, qn, kn, cos, sin, seq):
    n = xa.shape[0] + xb.shape[0]
    tm = TM_IN
    tiles_a = xa.shape[0] // tm
    tiles_per_seq = seq // tm
    row = lambda i: (i, 0)
    const = lambda i: (0, 0)
    pos = lambda i: (i % tiles_per_seq, 0)
    outs = [
        (2 * GLA_QK, F32), (GLA_V, BF16), (GLA_V, F32), (ATT_Q, BF16),
        (ATT_KV, BF16), (ATT_KV, BF16), (2 * D_MODEL, F32), (LANE, F32),
    ]
    return pl.pallas_call(
        functools.partial(_inproj_kernel, tiles_a),
        grid=(n // tm,),
        in_specs=[
            *_group_specs(tm, D_MODEL, tiles_a),
            pl.BlockSpec((1, D_MODEL), const),
            pl.BlockSpec((D_MODEL, C_END), const, pipeline_mode=pl.Buffered(1)),
            pl.BlockSpec((1, ATT_HD), const),
            pl.BlockSpec((1, ATT_HD), const),
            pl.BlockSpec((tm, ATT_HD), pos),
            pl.BlockSpec((tm, ATT_HD), pos),
        ],
        out_specs=[pl.BlockSpec((tm, w_), row) for w_, _ in outs],
        out_shape=[jax.ShapeDtypeStruct((n, w_), dt) for w_, dt in outs],
        compiler_params=_params("parallel"),
        name="inproj",
    )(xa, xb, ln, w, qn, kn, cos, sin)


def _log_sigmoid(u):
    return jnp.minimum(u, 0.0) - jnp.log(1.0 + jnp.exp(-jnp.abs(u)))


def _gla_chain(d, g, q, k, v, st, tri, keep):
    c = GLA_CHUNK
    nc = g.shape[0] // c
    chunk = lambda t, i: t[i * c:(i + 1) * c]
    g_hi = g.astype(BF16)
    g_lo = (g - g_hi.astype(F32)).astype(BF16)
    g_hl = jnp.concatenate([g_hi, g_lo], axis=1)
    b_parts, tot_parts, tot_rows = [], [], []
    for i in range(nc):
        b2 = _dot(tri, chunk(g_hl, i))
        b_i = b2[:, :GLA_DK] + b2[:, GLA_DK:]
        t_i = b_i[c - 1:c] if d == 0 else b_i[0:1]
        b_parts.append(b_i)
        tot_rows.append(t_i)
        tot_parts.append(jnp.broadcast_to(t_i, (c, GLA_DK)))
    yield
    b = jnp.concatenate(b_parts, axis=0)
    b_tot = jnp.concatenate(tot_parts, axis=0)
    qg = (q * jnp.exp(b)).astype(BF16)
    kg = (k * jnp.exp(-b)).astype(BF16)
    k_end = (k * jnp.exp(b_tot - b)).astype(BF16)
    yield
    a, u, dec = [], [], []
    for i in range(nc):
        a.append(jnp.where(keep, _dot_nt(chunk(qg, i), chunk(kg, i)), 0.0).astype(BF16))
        u.append(_dot_tn(chunk(k_end, i), chunk(v, i)))
        dec_col = jnp.broadcast_to(jnp.exp(tot_rows[i]), (GLA_DK, GLA_DK)).T
        dec.append(jnp.concatenate([dec_col] * (GLA_DV // GLA_DK), axis=1))
    yield
    st_in = [None] * nc
    for i in (range(nc) if d == 0 else range(nc - 1, -1, -1)):
        st_in[i] = st.astype(BF16)
        st = dec[i] * st + u[i]
    yield
    o = [_dot(jnp.concatenate([chunk(qg, i), a[i]], axis=1),
              jnp.concatenate([st_in[i], chunk(v, i)], axis=0)) for i in range(nc)]
    return jnp.concatenate(o, axis=0), st


def _run_interleaved(chains):
    results = [None] * len(chains)
    live = list(enumerate(chains))
    while live:
        still = []
        for idx, gen in live:
            try:
                next(gen)
                still.append((idx, gen))
            except StopIteration as stop:
                results[idx] = stop.value
        live = still
    return results


def _gla_kernel(q_ref, k_ref, v_ref, r_ref, z_ref, wf_ref, wb_ref, bf_ref, bb_ref, gn_ref,
                o_ref, gf_s, gb_s, st_s, oacc_s):
    seq = q_ref.shape[0]
    c = GLA_CHUNK
    br = GLA_BLOCK * c
    n_blocks = seq // br

    z = z_ref[...].astype(BF16)
    gf_s[...] = _log_sigmoid(_dot(z, wf_ref[...]) + bf_ref[...]) * (1.0 / GLA_TAU)
    gb_s[...] = _log_sigmoid(_dot(z, wb_ref[...]) + bb_ref[...]) * (1.0 / GLA_TAU)
    st_s[...] = jnp.zeros_like(st_s)
    oacc_s[...] = jnp.zeros_like(oacc_s)

    row = lax.broadcasted_iota(jnp.int32, (c, c), 0)
    col = lax.broadcasted_iota(jnp.int32, (c, c), 1)
    tri = (jnp.where(col <= row, 1.0, 0.0).astype(BF16), jnp.where(col >= row, 1.0, 0.0).astype(BF16))
    keep = (col <= row, col > row)
    g_s = (gf_s, gb_s)

    def step(n, carry):
        where, chains = [], []
        for d in range(2):
            block = n if d == 0 else n_blocks - 1 - n
            rows = pl.ds(pl.multiple_of(block * br, br), br)
            for h in range(GLA_HG):
                ksl = slice(h * GLA_DK, (h + 1) * GLA_DK)
                vsl = slice(h * GLA_DV, (h + 1) * GLA_DV)
                where.append((rows, vsl, d * GLA_HG + h))
                chains.append(_gla_chain(d, g_s[d][rows, ksl], q_ref[rows, ksl] * (GLA_DK ** -0.5),
                                         k_ref[rows, ksl], v_ref[rows, vsl], st_s[d * GLA_HG + h],
                                         tri[d], keep[d]))
        for (rows, vsl, slot), (o, st) in zip(where, _run_interleaved(chains)):
            oacc_s[rows, vsl] += o
            st_s[slot] = st
        return carry

    lax.fori_loop(0, n_blocks, step, 0)

    for h in range(GLA_HG):
        vsl = slice(h * GLA_DV, (h + 1) * GLA_DV)
        o = _rms(oacc_s[:, vsl], gn_ref[...])
        r = r_ref[:, vsl]
        o_ref[:, vsl] = (o * (r * jax.nn.sigmoid(r))).astype(BF16)


def _gla(gqk, gv, gr, z, wdec, bdec, gnorm, seq):
    n = gqk.shape[0]
    nb = n // seq
    hg = GLA_HG
    groups = GLA_HEADS // hg
    return pl.pallas_call(
        _gla_kernel,
        grid=(nb, groups),
        in_specs=[
            pl.BlockSpec((seq, hg * GLA_DK), lambda b, g: (b, g)),
            pl.BlockSpec((seq, hg * GLA_DK), lambda b, g: (b, groups + g)),
            pl.BlockSpec((seq, hg * GLA_DV), lambda b, g: (b, g)),
            pl.BlockSpec((seq, hg * GLA_DV), lambda b, g: (b, g)),
            pl.BlockSpec((seq, LANE), lambda b, g: (b, 0)),
            pl.BlockSpec((LANE, hg * GLA_DK), lambda b, g: (0, g)),
            pl.BlockSpec((LANE, hg * GLA_DK), lambda b, g: (0, groups + g)),
            pl.BlockSpec((1, hg * GLA_DK), lambda b, g: (0, g)),
            pl.BlockSpec((1, hg * GLA_DK), lambda b, g: (0, groups + g)),
            pl.BlockSpec((1, GLA_DV), lambda b, g: (0, 0)),
        ],
        out_specs=pl.BlockSpec((seq, hg * GLA_DV), lambda b, g: (b, g)),
        out_shape=jax.ShapeDtypeStruct((n, GLA_V), BF16),
        scratch_shapes=[
            pltpu.VMEM((seq, hg * GLA_DK), F32),
            pltpu.VMEM((seq, hg * GLA_DK), F32),
            pltpu.VMEM((2 * hg, GLA_DK, GLA_DV), F32),
            pltpu.VMEM((seq, hg * GLA_DV), F32),
        ],
        compiler_params=_params("parallel", "parallel"),
        name="gla",
    )(gqk, gqk, gv, gr, z, wdec, wdec, bdec, bdec, gnorm)


def _attn_kernel(q_ref, k_ref, v_ref, o_ref, s_s, m_s, p_s, l_s):
    hd = ATT_HD
    tq = Q_BLOCK
    nq = q_ref.shape[0] // tq

    def rows_of(i):
        return pl.ds(pl.multiple_of(i * tq, tq), tq)

    def scores(i, slot):
        rows = rows_of(i)
        q = jnp.concatenate([q_ref[rows, g * hd:(g + 1) * hd] for g in range(ATT_GROUP)], axis=0)
        s = _dot_nt(q, k_ref[...])
        m_s[slot] = jnp.max(s, axis=-1, keepdims=True)
        s_s[slot] = s

    def softmax(slot):
        p = jnp.exp(s_s[slot] - m_s[slot])
        l_s[slot] = jnp.sum(p, axis=-1, keepdims=True)
        p_s[slot] = p.astype(BF16)

    def values(i, slot):
        rows = rows_of(i)
        o = _dot(p_s[slot], v_ref[...]) / l_s[slot]
        for g in range(ATT_GROUP):
            o_ref[rows, g * hd:(g + 1) * hd] = o[g * tq:(g + 1) * tq].astype(BF16)

    scores(0, 0)
    softmax(0)
    scores(1, 1)

    def pair(j, carry):
        values(2 * j, 0)
        softmax(1)
        scores(2 * j + 2, 0)
        values(2 * j + 1, 1)
        softmax(0)
        scores(2 * j + 3, 1)
        return carry

    lax.fori_loop(0, (nq - 2) // 2, pair, 0)
    values(nq - 2, 0)
    softmax(1)
    values(nq - 1, 1)


def _attn(aq, ak, av, seq):
    n = aq.shape[0]
    nb = n // seq
    assert (seq // Q_BLOCK) % 2 == 0
    qw = ATT_GROUP * ATT_HD
    return pl.pallas_call(
        _attn_kernel,
        grid=(nb, ATT_KV_HEADS),
        in_specs=[
            pl.BlockSpec((seq, qw), lambda b, j: (b, j)),
            pl.BlockSpec((seq, ATT_HD), lambda b, j: (b, j)),
            pl.BlockSpec((seq, ATT_HD), lambda b, j: (b, j)),
        ],
        out_specs=pl.BlockSpec((seq, qw), lambda b, j: (b, j)),
        out_shape=jax.ShapeDtypeStruct((n, ATT_Q), BF16),
        scratch_shapes=[
            pltpu.VMEM((2, ATT_GROUP * Q_BLOCK, seq), F32),
            pltpu.VMEM((2, ATT_GROUP * Q_BLOCK, 1), F32),
            pltpu.VMEM((2, ATT_GROUP * Q_BLOCK, seq), BF16),
            pltpu.VMEM((2, ATT_GROUP * Q_BLOCK, 1), F32),
        ],
        compiler_params=_params("parallel", "parallel"),
        name="attn",
    )(aq, ak, av)


def _mix_kernel(tiles_a, xa_ref, xb_ref, og_ref, oa_ref, gate_ref, wbg_ref, wba_ref, wout_ref,
                lnpost_ref, lnx_ref, wq_ref, x1_ref, qx_ref):
    out_a = _dot(og_ref[...], wbg_ref[...])
    out_b = _dot(oa_ref[...], wba_ref[...])
    mixed = (jax.nn.sigmoid(gate_ref[:, :D_MODEL]) * out_a
             + jax.nn.sigmoid(gate_ref[:, D_MODEL:]) * out_b)
    y = _dot(mixed.astype(BF16), wout_ref[...])
    x1 = _group_read(tiles_a, xa_ref, xb_ref) + _rms(y, lnpost_ref[...])
    x1_ref[...] = x1
    h = _rms(x1, lnx_ref[...]).astype(BF16)
    qx_ref[...] = (_dot(h, wq_ref[...]) * (X_HD ** -0.5)).astype(BF16)


def _mix(xa, xb, og, oa, gate, wbg, wba, wout, lnpost, lnx, wq):
    n = xa.shape[0] + xb.shape[0]
    tm = TM_MIX
    tiles_a = xa.shape[0] // tm
    row = lambda i: (i, 0)
    const = lambda i: (0, 0)
    sq = pl.BlockSpec((D_MODEL, D_MODEL), const)
    vec = pl.BlockSpec((1, D_MODEL), const)
    return pl.pallas_call(
        functools.partial(_mix_kernel, tiles_a),
        grid=(n // tm,),
        in_specs=[
            *_group_specs(tm, D_MODEL, tiles_a), pl.BlockSpec((tm, GLA_V), row),
            pl.BlockSpec((tm, ATT_Q), row), pl.BlockSpec((tm, 2 * D_MODEL), row),
            sq, sq, sq, vec, vec, sq,
        ],
        out_specs=[pl.BlockSpec((tm, D_MODEL), row), pl.BlockSpec((tm, D_MODEL), row)],
        out_shape=[jax.ShapeDtypeStruct((n, D_MODEL), F32), jax.ShapeDtypeStruct((n, D_MODEL), BF16)],
        compiler_params=_params("parallel"),
        name="mix",
    )(xa, xb, og, oa, gate, wbg, wba, wout, lnpost, lnx, wq)


def _memkv_kernel(m_ref, ln_ref, w_ref, kv_ref):
    m = _rms(m_ref[...], ln_ref[...]).astype(BF16)
    kv_ref[...] = _dot(m, w_ref[...]).astype(BF16)


def _memkv(mem, ln, wkv):
    n = mem.shape[0]
    tm = 256
    return pl.pallas_call(
        _memkv_kernel,
        grid=(n // tm,),
        in_specs=[
            pl.BlockSpec((tm, D_MODEL), lambda i: (i, 0)),
            pl.BlockSpec((1, D_MODEL), lambda i: (0, 0)),
            pl.BlockSpec((D_MODEL, 2 * D_MODEL), lambda i: (0, 0)),
        ],
        out_specs=pl.BlockSpec((tm, 2 * D_MODEL), lambda i: (i, 0)),
        out_shape=jax.ShapeDtypeStruct((n, 2 * D_MODEL), BF16),
        compiler_params=_params("parallel"),
        name="memkv",
    )(mem, ln, wkv)


def _xffn_kernel(tiles_a, x1_ref, qx_ref, kv_ref, wo_ref, lnxpost_ref, lnffn_ref, wi_ref, wo2_ref,
                 lnpost_ref, ya_ref, yb_ref):
    heads = []
    for j in range(X_HEADS):
        ksl = slice(j * X_HD, (j + 1) * X_HD)
        vsl = slice(D_MODEL + j * X_HD, D_MODEL + (j + 1) * X_HD)
        s = _dot_nt(qx_ref[:, ksl], kv_ref[:, ksl])
        p = jnp.exp(s - jnp.max(s, axis=-1, keepdims=True))
        l = jnp.sum(p, axis=-1, keepdims=True)
        heads.append((_dot(p.astype(BF16), kv_ref[:, vsl]) / l).astype(BF16))
    o = jnp.concatenate(heads, axis=1)
    x2 = x1_ref[...] + _rms(_dot(o, wo_ref[...]), lnxpost_ref[...])
    h = _rms(x2, lnffn_ref[...]).astype(BF16)
    g = _dot(h, wi_ref[:, :D_FF])
    u = _dot(h, wi_ref[:, D_FF:])
    f = _dot((g * jax.nn.sigmoid(g) * u).astype(BF16), wo2_ref[...])
    y = x2 + _rms(f, lnpost_ref[...])
    i = pl.program_id(0)

    @pl.when(i < tiles_a)
    def _():
        ya_ref[...] = y

    @pl.when(i >= tiles_a)
    def _():
        yb_ref[...] = y


def _xffn(x1, qx, kv, wo, lnxpost, lnffn, wi, wo2, lnpost, seq, n_mem, n_a):
    n = x1.shape[0]
    tm = TM_X
    tiles_per_seq = seq // tm
    tiles_a = n_a // tm
    row = lambda i: (i, 0)
    const = lambda i: (0, 0)
    resident = lambda shape: pl.BlockSpec(shape, const, pipeline_mode=pl.Buffered(1))
    vec = pl.BlockSpec((1, D_MODEL), const)
    return pl.pallas_call(
        functools.partial(_xffn_kernel, tiles_a),
        grid=(n // tm,),
        in_specs=[
            pl.BlockSpec((tm, D_MODEL), row), pl.BlockSpec((tm, D_MODEL), row),
            pl.BlockSpec((n_mem, 2 * D_MODEL), lambda i: (i // tiles_per_seq, 0)),
            resident((D_MODEL, D_MODEL)), vec, vec,
            resident((D_MODEL, 2 * D_FF)), resident((D_FF, D_MODEL)), vec,
        ],
        out_specs=list(_group_specs(tm, D_MODEL, tiles_a)),
        out_shape=[jax.ShapeDtypeStruct((n_a, D_MODEL), F32), jax.ShapeDtypeStruct((n - n_a, D_MODEL), F32)],
        compiler_params=_params("arbitrary"),
        name="xffn",
    )(x1, qx, kv, wo, lnxpost, lnffn, wi, wo2, lnpost)


def _rope_tables(seq):
    rows = seq // GRID_W
    row = jnp.broadcast_to(jnp.arange(rows)[:, None], (rows, GRID_W)).reshape(-1).astype(F32)
    col = jnp.broadcast_to(jnp.arange(GRID_W)[None, :], (rows, GRID_W)).reshape(-1).astype(F32)
    half = ATT_HD // 2
    inv_freq = ROPE_THETA ** (-jnp.arange(0, half, 2, dtype=F32) / half)
    ang_r = row[:, None] * inv_freq[None, :]
    ang_c = col[:, None] * inv_freq[None, :]
    cr, sr, cc, sc = jnp.cos(ang_r), jnp.sin(ang_r), jnp.cos(ang_c), jnp.sin(ang_c)
    return (jnp.concatenate([cr, cr, cc, cc], axis=-1),
            jnp.concatenate([-sr, sr, -sc, sc], axis=-1))


def _pack_w_in(w_in):
    o = [0]
    for s in (GLA_QK, GLA_QK, GLA_V, GLA_V, GLA_RANK, GLA_RANK, ATT_Q, ATT_KV, ATT_KV, D_MODEL, D_MODEL):
        o.append(o[-1] + s)
    gq, gk, gv, gr, zf, zb, aq, ak, av, ga, gb = [w_in[:, o[i]:o[i + 1]] for i in range(11)]
    pad = jnp.zeros((D_MODEL, LANE - 2 * GLA_RANK), w_in.dtype)
    return jnp.concatenate([gq, gk, gv, gr, aq, ak, av, ga, gb, zf, zb, pad], axis=1).astype(BF16)


def _pack_decay(wa_f, ba_f, wa_b, ba_b):
    w = jnp.zeros((LANE, 2 * GLA_QK), F32)
    w = w.at[:GLA_RANK, :GLA_QK].set(wa_f).at[GLA_RANK:2 * GLA_RANK, GLA_QK:].set(wa_b)
    return w.astype(BF16), jnp.concatenate([ba_f, ba_b])[None, :]


def _layer(xa, xb, mem, seq, n_mem, ln_mix_pre, w_in, gla_wa_f, gla_ba_f, gla_wa_b, gla_ba_b, gla_norm,
           att_q_norm, att_k_norm, w_branch_gla, w_branch_att, w_out, ln_mix_post,
           ln_x_pre, ln_mem, x_wq, x_wkv, x_wo, ln_x_post, ln_ffn_pre, ffn_wi, ffn_wo, ln_ffn_post):
    vec = lambda v: v[None, :]
    cos, sin = _rope_tables(seq)
    wdec, bdec = _pack_decay(gla_wa_f, gla_ba_f, gla_wa_b, gla_ba_b)
    gqk, gv, gr, aq, ak, av, gate, z = _inproj(
        xa, xb, vec(ln_mix_pre), _pack_w_in(w_in), vec(att_q_norm), vec(att_k_norm), cos, sin, seq)
    og = _gla(gqk, gv, gr, z, wdec, bdec, vec(gla_norm), seq)
    oa = _attn(aq, ak, av, seq)
    x1, qx = _mix(xa, xb, og, oa, gate, w_branch_gla.astype(BF16), w_branch_att.astype(BF16),
                  w_out.astype(BF16), vec(ln_mix_post), vec(ln_x_pre), x_wq.astype(BF16))
    kv = _memkv(mem, vec(ln_mem), x_wkv.astype(BF16))
    return _xffn(x1, qx, kv, x_wo.astype(BF16), vec(ln_x_post), vec(ln_ffn_pre), ffn_wi.astype(BF16),
                 ffn_wo.astype(BF16), vec(ln_ffn_post), seq, n_mem, xa.shape[0])


def kernel(x_prompt, x_sample, mem_prompt, mem_sample, ln_mix_pre, w_in, gla_wa_f, gla_ba_f, gla_wa_b, gla_ba_b, gla_norm, att_q_norm, att_k_norm, w_branch_gla, w_branch_att, w_out, ln_mix_post, ln_x_pre, ln_mem, x_wq, x_wkv, x_wo, ln_x_post, ln_ffn_pre, ffn_wi, ffn_wo, ln_ffn_post):
    params = (ln_mix_pre, w_in, gla_wa_f, gla_ba_f, gla_wa_b, gla_ba_b, gla_norm,
              att_q_norm, att_k_norm, w_branch_gla, w_branch_att, w_out, ln_mix_post,
              ln_x_pre, ln_mem, x_wq, x_wkv, x_wo, ln_x_post,
              ln_ffn_pre, ffn_wi, ffn_wo, ln_ffn_post)
    bp, seq, d = x_prompt.shape
    bs = x_sample.shape[0]
    n_mem = mem_prompt.shape[1]
    assert x_sample.shape[1:] == (seq, d) and mem_sample.shape[1:] == (n_mem, d) and d == D_MODEL
    xa = x_prompt.reshape(bp * seq, d)
    xb = x_sample.reshape(bs * seq, d)
    mem = jnp.concatenate([mem_prompt, mem_sample], axis=0).reshape((bp + bs) * n_mem, d)
    for l in range(ln_mix_pre.shape[0]):
        xa, xb = _layer(xa, xb, mem, seq, n_mem, *[p[l] for p in params])
    return (xa.reshape(bp, seq, d), xb.reshape(bs, seq, d))
```

```python
import functools

import jax
import jax.numpy as jnp
from jax import lax
from jax.experimental import pallas as pl
from jax.experimental.pallas import tpu as pltpu

F32 = jnp.float32
BF16 = jnp.bfloat16

D_MODEL = 1024
GRID_W = 64
EPS = 1e-6
GLA_HEADS = 4
GLA_DK = 128
GLA_DV = 256
GLA_QK = GLA_HEADS * GLA_DK
GLA_V = GLA_HEADS * GLA_DV
GLA_RANK = 16
GLA_TAU = 16.0
GLA_CHUNK = 64
ATT_HEADS = 8
ATT_KV_HEADS = 2
ATT_GROUP = ATT_HEADS // ATT_KV_HEADS
ATT_HD = 128
ATT_Q = ATT_HEADS * ATT_HD
ATT_KV = ATT_KV_HEADS * ATT_HD
ROPE_THETA = 10000.0
Q_BLOCK = 128
X_HEADS = 4
X_HD = D_MODEL // X_HEADS
D_FF = 2816

LOG2E = 1.4426950408889634
LANE = 128
VMEM_LIMIT = 56 * 1024 * 1024

C_GQK = 0
C_GV = C_GQK + 2 * GLA_QK
C_GR = C_GV + GLA_V
C_AQ = C_GR + GLA_V
C_AK = C_AQ + ATT_Q
C_AV = C_AK + ATT_KV
C_GATE = C_AV + ATT_KV
C_Z = C_GATE + 2 * D_MODEL
C_END = C_Z + LANE

TM_IN = 256
TM_MIX = 512
TM_X = 512
GLA_HG = 2
GLA_BLOCK = 8


def _rms(x, w):
    return x * lax.rsqrt(jnp.mean(x * x, axis=-1, keepdims=True) + EPS) * w


def _dot(a, b):
    return jnp.dot(a, b, preferred_element_type=F32)


def _dot_nt(a, b):
    return lax.dot_general(a, b, (((1,), (1,)), ((), ())), preferred_element_type=F32)


def _dot_tn(a, b):
    return lax.dot_general(a, b, (((0,), (0,)), ((), ())), preferred_element_type=F32)


def _params(*sem):
    return pltpu.CompilerParams(dimension_semantics=sem, vmem_limit_bytes=VMEM_LIMIT)


def _group_specs(tm, width, tiles_a):
    return (pl.BlockSpec((tm, width), lambda i: (jnp.minimum(i, tiles_a - 1), 0)),
            pl.BlockSpec((tm, width), lambda i: (jnp.maximum(i - tiles_a, 0), 0)))


def _group_read(tiles_a, a_ref, b_ref):
    return jnp.where(pl.program_id(0) < tiles_a, a_ref[...], b_ref[...])


def _inproj_kernel(tiles_a, xa_ref, xb_ref, ln_ref, w_ref, qn_ref, kn_ref, cos_ref, sin_ref,
                   gqk_ref, gv_ref, gr_ref, aq_ref, ak_ref, av_ref, gate_ref, z_ref):
    h = _rms(_group_read(tiles_a, xa_ref, xb_ref), ln_ref[...]).astype(BF16)

    def proj(lo, hi):
        return _dot(h, w_ref[:, lo:hi])

    gqk_ref[...] = proj(C_GQK, C_GV)
    gv_ref[...] = proj(C_GV, C_GR).astype(BF16)
    gr_ref[...] = proj(C_GR, C_AQ)

    cos = cos_ref[...]
    sin = sin_ref[...]
    lane = lax.broadcasted_iota(jnp.int32, cos.shape, 1)
    first_half = (lane % 64) < 32

    def norm_rope(y, gain, scale):
        yn = _rms(y, gain)
        partner = jnp.where(first_half, pltpu.roll(yn, LANE - 32, 1), pltpu.roll(yn, 32, 1))
        return ((yn * cos + partner * sin) * scale).astype(BF16)

    aq = proj(C_AQ, C_AK)
    for j in range(ATT_HEADS):
        sl = slice(j * ATT_HD, (j + 1) * ATT_HD)
        aq_ref[:, sl] = norm_rope(aq[:, sl], qn_ref[...], ATT_HD ** -0.5 * LOG2E)
    ak = proj(C_AK, C_AV)
    for j in range(ATT_KV_HEADS):
        sl = slice(j * ATT_HD, (j + 1) * ATT_HD)
        ak_ref[:, sl] = norm_rope(ak[:, sl], kn_ref[...], 1.0)
    av_ref[...] = proj(C_AV, C_GATE).astype(BF16)
    gate_ref[...] = proj(C_GATE, C_Z)
    z_ref[...] = proj(C_Z, C_END)


def _inproj(xa, xb, ln, w, qn, kn, cos, sin, seq):
    n = xa.shape[0] + xb.shape[0]
    tm = TM_IN
    tiles_a = xa.shape[0] // tm
    tiles_per_seq = seq // tm
    row = lambda i: (i, 0)
    const = lambda i: (0, 0)
    pos = lambda i: (i % tiles_per_seq, 0)
    outs = [
        (2 * GLA_QK, F32), (GLA_V, BF16), (GLA_V, F32), (ATT_Q, BF16),
        (ATT_KV, BF16), (ATT_KV, BF16), (2 * D_MODEL, F32), (LANE, F32),
    ]
    return pl.pallas_call(
        functools.partial(_inproj_kernel, tiles_a),
        grid=(n // tm,),
        in_specs=[
            *_group_specs(tm, D_MODEL, tiles_a),
            pl.BlockSpec((1, D_MODEL), const),
            pl.BlockSpec((D_MODEL, C_END), const, pipeline_mode=pl.Buffered(1)),
            pl.BlockSpec((1, ATT_HD), const),
            pl.BlockSpec((1, ATT_HD), const),
            pl.BlockSpec((tm, ATT_HD), pos),
            pl.BlockSpec((tm, ATT_HD), pos),
        ],
        out_specs=[pl.BlockSpec((tm, w_), row) for w_, _ in outs],
        out_shape=[jax.ShapeDtypeStruct((n, w_), dt) for w_, dt in outs],
        compiler_params=_params("parallel"),
        name="inproj",
    )(xa, xb, ln, w, qn, kn, cos, sin)


def _log_sigmoid(u):
    return jnp.minimum(u, 0.0) - jnp.log(1.0 + jnp.exp(-jnp.abs(u)))


def _gla_chain(d, g, q, k, v, st, tri, keep):
    c = GLA_CHUNK
    nc = g.shape[0] // c
    chunk = lambda t, i: t[i * c:(i + 1) * c]
    g_hi = g.astype(BF16)
    g_lo = (g - g_hi.astype(F32)).astype(BF16)
    g_hl = jnp.concatenate([g_hi, g_lo], axis=1)
    b_parts, tot_parts, tot_rows = [], [], []
    for i in range(nc):
        b2 = _dot(tri, chunk(g_hl, i))
        b_i = b2[:, :GLA_DK] + b2[:, GLA_DK:]
        t_i = b_i[c - 1:c] if d == 0 else b_i[0:1]
        b_parts.append(b_i)
        tot_rows.append(t_i)
        tot_parts.append(jnp.broadcast_to(t_i, (c, GLA_DK)))
    yield
    b = jnp.concatenate(b_parts, axis=0)
    b_tot = jnp.concatenate(tot_parts, axis=0)
    qg = (q * jnp.exp(b)).astype(BF16)
    kg = (k * jnp.exp(-b)).astype(BF16)
    k_end = (k * jnp.exp(b_tot - b)).astype(BF16)
    yield
    a = [jnp.where(keep, _dot_nt(chunk(qg, i), chunk(kg, i)), 0.0).astype(BF16) for i in range(nc)]
    dec = [jnp.broadcast_to(jnp.exp(tot_rows[i]), (GLA_DK, GLA_DK)).T for i in range(nc)]
    yield
    st_in = [None] * nc
    for i in (range(nc) if d == 0 else range(nc - 1, -1, -1)):
        st_in[i] = st.astype(BF16)
        u = _dot_tn(chunk(k_end, i), chunk(v, i))
        st = jnp.concatenate([dec[i] * st[:, j:j + GLA_DK] for j in range(0, GLA_DV, GLA_DK)], axis=1) + u
    yield
    o = [_dot(jnp.concatenate([chunk(qg, i), a[i]], axis=1),
              jnp.concatenate([st_in[i], chunk(v, i)], axis=0)) for i in range(nc)]
    return jnp.concatenate(o, axis=0), st


def _run_interleaved(chains):
    results = [None] * len(chains)
    live = list(enumerate(chains))
    while live:
        still = []
        for idx, gen in live:
            try:
                next(gen)
                still.append((idx, gen))
            except StopIteration as stop:
                results[idx] = stop.value
        live = still
    return results


def _gla_kernel(q_ref, k_ref, v_ref, r_ref, z_ref, wf_ref, wb_ref, bf_ref, bb_ref, gn_ref,
                o_ref, gf_s, gb_s, st_s, oacc_s):
    seq = q_ref.shape[0]
    c = GLA_CHUNK
    br = GLA_BLOCK * c
    n_blocks = seq // br

    z = z_ref[...].astype(BF16)
    gf_s[...] = _log_sigmoid(_dot(z, wf_ref[...]) + bf_ref[...]) * (1.0 / GLA_TAU)
    gb_s[...] = _log_sigmoid(_dot(z, wb_ref[...]) + bb_ref[...]) * (1.0 / GLA_TAU)
    st_s[...] = jnp.zeros_like(st_s)
    oacc_s[...] = jnp.zeros_like(oacc_s)

    row = lax.broadcasted_iota(jnp.int32, (c, c), 0)
    col = lax.broadcasted_iota(jnp.int32, (c, c), 1)
    tri = (jnp.where(col <= row, 1.0, 0.0).astype(BF16), jnp.where(col >= row, 1.0, 0.0).astype(BF16))
    keep = (col <= row, col > row)
    g_s = (gf_s, gb_s)

    def step(n, carry):
        where, chains = [], []
        for d in range(2):
            block = n if d == 0 else n_blocks - 1 - n
            rows = pl.ds(pl.multiple_of(block * br, br), br)
            for h in range(GLA_HG):
                ksl = slice(h * GLA_DK, (h + 1) * GLA_DK)
                vsl = slice(h * GLA_DV, (h + 1) * GLA_DV)
                where.append((rows, vsl, d * GLA_HG + h))
                chains.append(_gla_chain(d, g_s[d][rows, ksl], q_ref[rows, ksl] * (GLA_DK ** -0.5),
                                         k_ref[rows, ksl], v_ref[rows, vsl], st_s[d * GLA_HG + h],
                                         tri[d], keep[d]))
        for (rows, vsl, slot), (o, st) in zip(where, _run_interleaved(chains)):
            oacc_s[rows, vsl] += o
            st_s[slot] = st
        return carry

    lax.fori_loop(0, n_blocks, step, 0)

    for h in range(GLA_HG):
        vsl = slice(h * GLA_DV, (h + 1) * GLA_DV)
        o = _rms(oacc_s[:, vsl], gn_ref[...])
        r = r_ref[:, vsl]
        o_ref[:, vsl] = (o * (r * jax.nn.sigmoid(r))).astype(BF16)


def _gla(gqk, gv, gr, z, wdec, bdec, gnorm, seq):
    n = gqk.shape[0]
    nb = n // seq
    hg = GLA_HG
    groups = GLA_HEADS // hg
    return pl.pallas_call(
        _gla_kernel,
        grid=(nb, groups),
        in_specs=[
            pl.BlockSpec((seq, hg * GLA_DK), lambda b, g: (b, g)),
            pl.BlockSpec((seq, hg * GLA_DK), lambda b, g: (b, groups + g)),
            pl.BlockSpec((seq, hg * GLA_DV), lambda b, g: (b, g)),
            pl.BlockSpec((seq, hg * GLA_DV), lambda b, g: (b, g)),
            pl.BlockSpec((seq, LANE), lambda b, g: (b, 0)),
            pl.BlockSpec((LANE, hg * GLA_DK), lambda b, g: (0, g)),
            pl.BlockSpec((LANE, hg * GLA_DK), lambda b, g: (0, groups + g)),
            pl.BlockSpec((1, hg * GLA_DK), lambda b, g: (0, g)),
            pl.BlockSpec((1, hg * GLA_DK), lambda b, g: (0, groups + g)),
            pl.BlockSpec((1, GLA_DV), lambda b, g: (0, 0)),
        ],
        out_specs=pl.BlockSpec((seq, hg * GLA_DV), lambda b, g: (b, g)),
        out_shape=jax.ShapeDtypeStruct((n, GLA_V), BF16),
        scratch_shapes=[
            pltpu.VMEM((seq, hg * GLA_DK), F32),
            pltpu.VMEM((seq, hg * GLA_DK), F32),
            pltpu.VMEM((2 * hg, GLA_DK, GLA_DV), F32),
            pltpu.VMEM((seq, hg * GLA_DV), F32),
        ],
        compiler_params=_params("parallel", "parallel"),
        name="gla",
    )(gqk, gqk, gv, gr, z, wdec, wdec, bdec, bdec, gnorm)


def _attn_kernel(q_ref, k_ref, v_ref, o_ref, s0, s1, m0, m1, p0, p1, v1_s):
    s_s, m_s, p_s = (s0, s1), (m0, m1), (p0, p1)
    hd = ATT_HD
    tq = Q_BLOCK
    nq = q_ref.shape[0] // tq
    v1_s[:, :hd] = v_ref[...]
    v1_s[:, hd:] = jnp.ones((v1_s.shape[0], hd), BF16)

    def rows_of(i):
        return pl.ds(pl.multiple_of(i * tq, tq), tq)

    def scores(i, slot):
        rows = rows_of(i)
        q = jnp.concatenate([q_ref[rows, g * hd:(g + 1) * hd] for g in range(ATT_GROUP)], axis=0)
        s = _dot_nt(q, k_ref[...])
        m_s[slot][...] = jnp.max(s, axis=-1, keepdims=True)
        s_s[slot][...] = s

    def softmax(slot):
        p_s[slot][...] = jnp.exp2(s_s[slot][...] - m_s[slot][...]).astype(BF16)

    def values(i, slot):
        rows = rows_of(i)
        ol = _dot(p_s[slot][...], v1_s[...])
        o = ol[:, :hd] / ol[:, hd:]
        for g in range(ATT_GROUP):
            o_ref[rows, g * hd:(g + 1) * hd] = o[g * tq:(g + 1) * tq].astype(BF16)

    scores(0, 0)
    softmax(0)
    scores(1, 1)

    def pair(j, carry):
        values(2 * j, 0)
        softmax(1)
        scores(2 * j + 2, 0)
        values(2 * j + 1, 1)
        softmax(0)
        scores(2 * j + 3, 1)
        return carry

    lax.fori_loop(0, (nq - 2) // 2, pair, 0)
    values(nq - 2, 0)
    softmax(1)
    values(nq - 1, 1)


def _attn(aq, ak, av, seq):
    n = aq.shape[0]
    nb = n // seq
    assert (seq // Q_BLOCK) % 2 == 0
    qw = ATT_GROUP * ATT_HD
    return pl.pallas_call(
        _attn_kernel,
        grid=(nb, ATT_KV_HEADS),
        in_specs=[
            pl.BlockSpec((seq, qw), lambda b, j: (b, j)),
            pl.BlockSpec((seq, ATT_HD), lambda b, j: (b, j)),
            pl.BlockSpec((seq, ATT_HD), lambda b, j: (b, j)),
        ],
        out_specs=pl.BlockSpec((seq, qw), lambda b, j: (b, j)),
        out_shape=jax.ShapeDtypeStruct((n, ATT_Q), BF16),
        scratch_shapes=[
            pltpu.VMEM((ATT_GROUP * Q_BLOCK, seq), F32), pltpu.VMEM((ATT_GROUP * Q_BLOCK, seq), F32),
            pltpu.VMEM((ATT_GROUP * Q_BLOCK, 1), F32), pltpu.VMEM((ATT_GROUP * Q_BLOCK, 1), F32),
            pltpu.VMEM((ATT_GROUP * Q_BLOCK, seq), BF16), pltpu.VMEM((ATT_GROUP * Q_BLOCK, seq), BF16),
            pltpu.VMEM((seq, 2 * ATT_HD), BF16),
        ],
        compiler_params=_params("parallel", "parallel"),
        name="attn",
    )(aq, ak, av)


def _mix_kernel(tiles_a, xa_ref, xb_ref, og_ref, oa_ref, gate_ref, wbg_ref, wba_ref, wout_ref,
                lnpost_ref, lnx_ref, wq_ref, x1_ref, qx_ref):
    out_a = _dot(og_ref[...], wbg_ref[...])
    out_b = _dot(oa_ref[...], wba_ref[...])
    mixed = (jax.nn.sigmoid(gate_ref[:, :D_MODEL]) * out_a
             + jax.nn.sigmoid(gate_ref[:, D_MODEL:]) * out_b)
    y = _dot(mixed.astype(BF16), wout_ref[...])
    x1 = _group_read(tiles_a, xa_ref, xb_ref) + _rms(y, lnpost_ref[...])
    x1_ref[...] = x1
    h = _rms(x1, lnx_ref[...]).astype(BF16)
    qx_ref[...] = (_dot(h, wq_ref[...]) * (X_HD ** -0.5)).astype(BF16)


def _mix(xa, xb, og, oa, gate, wbg, wba, wout, lnpost, lnx, wq):
    n = xa.shape[0] + xb.shape[0]
    tm = TM_MIX
    tiles_a = xa.shape[0] // tm
    row = lambda i: (i, 0)
    const = lambda i: (0, 0)
    sq = pl.BlockSpec((D_MODEL, D_MODEL), const)
    vec = pl.BlockSpec((1, D_MODEL), const)
    return pl.pallas_call(
        functools.partial(_mix_kernel, tiles_a),
        grid=(n // tm,),
        in_specs=[
            *_group_specs(tm, D_MODEL, tiles_a), pl.BlockSpec((tm, GLA_V), row),
            pl.BlockSpec((tm, ATT_Q), row), pl.BlockSpec((tm, 2 * D_MODEL), row),
            sq, sq, sq, vec, vec, sq,
        ],
        out_specs=[pl.BlockSpec((tm, D_MODEL), row), pl.BlockSpec((tm, D_MODEL), row)],
        out_shape=[jax.ShapeDtypeStruct((n, D_MODEL), F32), jax.ShapeDtypeStruct((n, D_MODEL), BF16)],
        compiler_params=_params("parallel"),
        name="mix",
    )(xa, xb, og, oa, gate, wbg, wba, wout, lnpost, lnx, wq)


def _memkv_kernel(m_ref, ln_ref, w_ref, kv_ref):
    m = _rms(m_ref[...], ln_ref[...]).astype(BF16)
    kv_ref[...] = _dot(m, w_ref[...]).astype(BF16)


def _memkv(mem, ln, wkv):
    n = mem.shape[0]
    tm = 256
    return pl.pallas_call(
        _memkv_kernel,
        grid=(n // tm,),
        in_specs=[
            pl.BlockSpec((tm, D_MODEL), lambda i: (i, 0)),
            pl.BlockSpec((1, D_MODEL), lambda i: (0, 0)),
            pl.BlockSpec((D_MODEL, 2 * D_MODEL), lambda i: (0, 0)),
        ],
        out_specs=pl.BlockSpec((tm, 2 * D_MODEL), lambda i: (i, 0)),
        out_shape=jax.ShapeDtypeStruct((n, 2 * D_MODEL), BF16),
        compiler_params=_params("parallel"),
        name="memkv",
    )(mem, ln, wkv)


def _xffn_kernel(tiles_a, x1_ref, qx_ref, kv_ref, wo_ref, lnxpost_ref, lnffn_ref, wi_ref, wo2_ref,
                 lnpost_ref, ya_ref, yb_ref):
    heads = []
    for j in range(X_HEADS):
        ksl = slice(j * X_HD, (j + 1) * X_HD)
        vsl = slice(D_MODEL + j * X_HD, D_MODEL + (j + 1) * X_HD)
        s = _dot_nt(qx_ref[:, ksl], kv_ref[:, ksl])
        p = jnp.exp(s - jnp.max(s, axis=-1, keepdims=True))
        l = jnp.sum(p, axis=-1, keepdims=True)
        heads.append((_dot(p.astype(BF16), kv_ref[:, vsl]) / l).astype(BF16))
    o = jnp.concatenate(heads, axis=1)
    x2 = x1_ref[...] + _rms(_dot(o, wo_ref[...]), lnxpost_ref[...])
    h = _rms(x2, lnffn_ref[...]).astype(BF16)
    g = _dot(h, wi_ref[:, :D_FF])
    u = _dot(h, wi_ref[:, D_FF:])
    f = _dot((g * jax.nn.sigmoid(g) * u).astype(BF16), wo2_ref[...])
    y = x2 + _rms(f, lnpost_ref[...])
    i = pl.program_id(0)

    @pl.when(i < tiles_a)
    def _():
        ya_ref[...] = y

    @pl.when(i >= tiles_a)
    def _():
        yb_ref[...] = y


def _xffn(x1, qx, kv, wo, lnxpost, lnffn, wi, wo2, lnpost, seq, n_mem, n_a):
    n = x1.shape[0]
    tm = TM_X
    tiles_per_seq = seq // tm
    tiles_a = n_a // tm
    row = lambda i: (i, 0)
    const = lambda i: (0, 0)
    resident = lambda shape: pl.BlockSpec(shape, const, pipeline_mode=pl.Buffered(1))
    vec = pl.BlockSpec((1, D_MODEL), const)
    return pl.pallas_call(
        functools.partial(_xffn_kernel, tiles_a),
        grid=(n // tm,),
        in_specs=[
            pl.BlockSpec((tm, D_MODEL), row), pl.BlockSpec((tm, D_MODEL), row),
            pl.BlockSpec((n_mem, 2 * D_MODEL), lambda i: (i // tiles_per_seq, 0)),
            resident((D_MODEL, D_MODEL)), vec, vec,
            resident((D_MODEL, 2 * D_FF)), resident((D_FF, D_MODEL)), vec,
        ],
        out_specs=list(_group_specs(tm, D_MODEL, tiles_a)),
        out_shape=[jax.ShapeDtypeStruct((n_a, D_MODEL), F32), jax.ShapeDtypeStruct((n - n_a, D_MODEL), F32)],
        compiler_params=_params("arbitrary"),
        name="xffn",
    )(x1, qx, kv, wo, lnxpost, lnffn, wi, wo2, lnpost)


def _rope_tables(seq):
    rows = seq // GRID_W
    row = jnp.broadcast_to(jnp.arange(rows)[:, None], (rows, GRID_W)).reshape(-1).astype(F32)
    col = jnp.broadcast_to(jnp.arange(GRID_W)[None, :], (rows, GRID_W)).reshape(-1).astype(F32)
    half = ATT_HD // 2
    inv_freq = ROPE_THETA ** (-jnp.arange(0, half, 2, dtype=F32) / half)
    ang_r = row[:, None] * inv_freq[None, :]
    ang_c = col[:, None] * inv_freq[None, :]
    cr, sr, cc, sc = jnp.cos(ang_r), jnp.sin(ang_r), jnp.cos(ang_c), jnp.sin(ang_c)
    return (jnp.concatenate([cr, cr, cc, cc], axis=-1),
            jnp.concatenate([-sr, sr, -sc, sc], axis=-1))


def _pack_w_in(w_in):
    o = [0]
    for s in (GLA_QK, GLA_QK, GLA_V, GLA_V, GLA_RANK, GLA_RANK, ATT_Q, ATT_KV, ATT_KV, D_MODEL, D_MODEL):
        o.append(o[-1] + s)
    gq, gk, gv, gr, zf, zb, aq, ak, av, ga, gb = [w_in[:, o[i]:o[i + 1]] for i in range(11)]
    pad = jnp.zeros((D_MODEL, LANE - 2 * GLA_RANK), w_in.dtype)
    return jnp.concatenate([gq, gk, gv, gr, aq, ak, av, ga, gb, zf, zb, pad], axis=1).astype(BF16)


def _pack_decay(wa_f, ba_f, wa_b, ba_b):
    w = jnp.zeros((LANE, 2 * GLA_QK), F32)
    w = w.at[:GLA_RANK, :GLA_QK].set(wa_f).at[GLA_RANK:2 * GLA_RANK, GLA_QK:].set(wa_b)
    return w.astype(BF16), jnp.concatenate([ba_f, ba_b])[None, :]


def _layer(xa, xb, mem, seq, n_mem, ln_mix_pre, w_in, gla_wa_f, gla_ba_f, gla_wa_b, gla_ba_b, gla_norm,
           att_q_norm, att_k_norm, w_branch_gla, w_branch_att, w_out, ln_mix_post,
           ln_x_pre, ln_mem, x_wq, x_wkv, x_wo, ln_x_post, ln_ffn_pre, ffn_wi, ffn_wo, ln_ffn_post):
    vec = lambda v: v[None, :]
    cos, sin = _rope_tables(seq)
    wdec, bdec = _pack_decay(gla_wa_f, gla_ba_f, gla_wa_b, gla_ba_b)
    gqk, gv, gr, aq, ak, av, gate, z = _inproj(
        xa, xb, vec(ln_mix_pre), _pack_w_in(w_in), vec(att_q_norm), vec(att_k_norm), cos, sin, seq)
    og = _gla(gqk, gv, gr, z, wdec, bdec, vec(gla_norm), seq)
    oa = _attn(aq, ak, av, seq)
    x1, qx = _mix(xa, xb, og, oa, gate, w_branch_gla.astype(BF16), w_branch_att.astype(BF16),
                  w_out.astype(BF16), vec(ln_mix_post), vec(ln_x_pre), x_wq.astype(BF16))
    kv = _memkv(mem, vec(ln_mem), x_wkv.astype(BF16))
    return _xffn(x1, qx, kv, x_wo.astype(BF16), vec(ln_x_post), vec(ln_ffn_pre), ffn_wi.astype(BF16),
                 ffn_wo.astype(BF16), vec(ln_ffn_post), seq, n_mem, xa.shape[0])


def kernel(x_prompt, x_sample, mem_prompt, mem_sample, ln_mix_pre, w_in, gla_wa_f, gla_ba_f, gla_wa_b, gla_ba_b, gla_norm, att_q_norm, att_k_norm, w_branch_gla, w_branch_att, w_out, ln_mix_post, ln_x_pre, ln_mem, x_wq, x_wkv, x_wo, ln_x_post, ln_ffn_pre, ffn_wi, ffn_wo, ln_ffn_post):
    params = (ln_mix_pre, w_in, gla_wa_f, gla_ba_f, gla_wa_b, gla_ba_b, gla_norm,
              att_q_norm, att_k_norm, w_branch_gla, w_branch_att, w_out, ln_mix_post,
              ln_x_pre, ln_mem, x_wq, x_wkv, x_wo, ln_x_post,
              ln_ffn_pre, ffn_wi, ffn_wo, ln_ffn_post)
    bp, seq, d = x_prompt.shape
    bs = x_sample.shape[0]
    n_mem = mem_prompt.shape[1]
    assert x_sample.shape[1:] == (seq, d) and mem_sample.shape[1:] == (n_mem, d) and d == D_MODEL
    xa = x_prompt.reshape(bp * seq, d)
    xb = x_sample.reshape(bs * seq, d)
    mem = jnp.concatenate([mem_prompt, mem_sample], axis=0).reshape((bp + bs) * n_mem, d)
    for l in range(ln_mix_pre.shape[0]):
        xa, xb = _layer(xa, xb, mem, seq, n_mem, *[p[l] for p in params])
    return (xa.reshape(bp, seq, d), xb.reshape(bs, seq, d))
```

```python
import functools

import jax
import jax.numpy as jnp
from jax import lax
from jax.experimental import pallas as pl
from jax.experimental.pallas import tpu as pltpu

F32 = jnp.float32
BF16 = jnp.bfloat16

D_MODEL = 1024
GRID_W = 64
EPS = 1e-6
GLA_HEADS = 4
GLA_DK = 128
GLA_DV = 256
GLA_QK = GLA_HEADS * GLA_DK
GLA_V = GLA_HEADS * GLA_DV
GLA_RANK = 16
GLA_TAU = 16.0
GLA_CHUNK = 64
ATT_HEADS = 8
ATT_KV_HEADS = 2
ATT_GROUP = ATT_HEADS // ATT_KV_HEADS
ATT_HD = 128
ATT_Q = ATT_HEADS * ATT_HD
ATT_KV = ATT_KV_HEADS * ATT_HD
ROPE_THETA = 10000.0
Q_BLOCK = 128
X_HEADS = 4
X_HD = D_MODEL // X_HEADS
D_FF = 2816

LOG2E = 1.4426950408889634
LANE = 128
VMEM_LIMIT = 56 * 1024 * 1024

C_GQK = 0
C_GV = C_GQK + 2 * GLA_QK
C_GR = C_GV + GLA_V
C_AQ = C_GR + GLA_V
C_AK = C_AQ + ATT_Q
C_AV = C_AK + ATT_KV
C_GATE = C_AV + ATT_KV
C_Z = C_GATE + 2 * D_MODEL
C_END = C_Z + LANE

TM_IN = 256
TM_MIX = 512
TM_X = 512
GLA_HG = 2
GLA_BLOCK = 8


def _rms(x, w):
    return x * lax.rsqrt(jnp.mean(x * x, axis=-1, keepdims=True) + EPS) * w


def _dot(a, b):
    return jnp.dot(a, b, preferred_element_type=F32)


def _dot_nt(a, b):
    return lax.dot_general(a, b, (((1,), (1,)), ((), ())), preferred_element_type=F32)


def _dot_tn(a, b):
    return lax.dot_general(a, b, (((0,), (0,)), ((), ())), preferred_element_type=F32)


def _params(*sem):
    return pltpu.CompilerParams(dimension_semantics=sem, vmem_limit_bytes=VMEM_LIMIT)


def _group_specs(tm, width, tiles_a):
    return (pl.BlockSpec((tm, width), lambda i: (jnp.minimum(i, tiles_a - 1), 0)),
            pl.BlockSpec((tm, width), lambda i: (jnp.maximum(i - tiles_a, 0), 0)))


def _group_read(tiles_a, a_ref, b_ref):
    return jnp.where(pl.program_id(0) < tiles_a, a_ref[...], b_ref[...])


def _inproj_kernel(tiles_a, xa_ref, xb_ref, ln_ref, w_ref, qn_ref, kn_ref, cos_ref, sin_ref,
                   gqk_ref, gv_ref, gr_ref, aq_ref, ak_ref, av_ref, gate_ref, z_ref):
    h = _rms(_group_read(tiles_a, xa_ref, xb_ref), ln_ref[...]).astype(BF16)

    def proj(lo, hi):
        return _dot(h, w_ref[:, lo:hi])

    gqk_ref[...] = proj(C_GQK, C_GV)
    gv_ref[...] = proj(C_GV, C_GR).astype(BF16)
    gr_ref[...] = proj(C_GR, C_AQ)

    cos = cos_ref[...]
    sin = sin_ref[...]
    lane = lax.broadcasted_iota(jnp.int32, cos.shape, 1)
    first_half = (lane % 64) < 32

    def norm_rope(y, gain, scale):
        yn = _rms(y, gain)
        partner = jnp.where(first_half, pltpu.roll(yn, LANE - 32, 1), pltpu.roll(yn, 32, 1))
        return ((yn * cos + partner * sin) * scale).astype(BF16)

    aq = proj(C_AQ, C_AK)
    for j in range(ATT_HEADS):
        sl = slice(j * ATT_HD, (j + 1) * ATT_HD)
        aq_ref[:, sl] = norm_rope(aq[:, sl], qn_ref[...], ATT_HD ** -0.5 * LOG2E)
    ak = proj(C_AK, C_AV)
    for j in range(ATT_KV_HEADS):
        sl = slice(j * ATT_HD, (j + 1) * ATT_HD)
        ak_ref[:, sl] = norm_rope(ak[:, sl], kn_ref[...], 1.0)
    av_ref[...] = proj(C_AV, C_GATE).astype(BF16)
    gate_ref[...] = proj(C_GATE, C_Z)
    z_ref[...] = proj(C_Z, C_END)


def _inproj(xa, xb, ln, w, qn, kn, cos, sin, seq):
    n = xa.shape[0] + xb.shape[0]
    tm = TM_IN
    tiles_a = xa.shape[0] // tm
    tiles_per_seq = seq // tm
    row = lambda i: (i, 0)
    const = lambda i: (0, 0)
    pos = lambda i: (i % tiles_per_seq, 0)
    outs = [
        (2 * GLA_QK, F32), (GLA_V, BF16), (GLA_V, F32), (ATT_Q, BF16),
        (ATT_KV, BF16), (ATT_KV, BF16), (2 * D_MODEL, F32), (LANE, F32),
    ]
    return pl.pallas_call(
        functools.partial(_inproj_kernel, tiles_a),
        grid=(n // tm,),
        in_specs=[
            *_group_specs(tm, D_MODEL, tiles_a),
            pl.BlockSpec((1, D_MODEL), const),
            pl.BlockSpec((D_MODEL, C_END), const, pipeline_mode=pl.Buffered(1)),
            pl.BlockSpec((1, ATT_HD), const),
            pl.BlockSpec((1, ATT_HD), const),
            pl.BlockSpec((tm, ATT_HD), pos),
            pl.BlockSpec((tm, ATT_HD), pos),
        ],
        out_specs=[pl.BlockSpec((tm, w_), row) for w_, _ in outs],
        out_shape=[jax.ShapeDtypeStruct((n, w_), dt) for w_, dt in outs],
        compiler_params=_params("parallel"),
        name="inproj",
    )(xa, xb, ln, w, qn, kn, cos, sin)


def _log_sigmoid(u):
    return jnp.minimum(u, 0.0) - jnp.log(1.0 + jnp.exp(-jnp.abs(u)))


def _gla_chain(d, g, q, k, v, st, tri, keep):
    c = GLA_CHUNK
    nc = g.shape[0] // c
    chunk = lambda t, i: t[i * c:(i + 1) * c]
    g_hi = g.astype(BF16)
    g_lo = (g - g_hi.astype(F32)).astype(BF16)
    g_hl = jnp.concatenate([g_hi, g_lo], axis=1)
    b_parts, tot_parts, tot_rows = [], [], []
    for i in range(nc):
        b2 = _dot(tri, chunk(g_hl, i))
        b_i = b2[:, :GLA_DK] + b2[:, GLA_DK:]
        t_i = b_i[c - 1:c] if d == 0 else b_i[0:1]
        b_parts.append(b_i)
        tot_rows.append(t_i)
        tot_parts.append(jnp.broadcast_to(t_i, (c, GLA_DK)))
    yield
    b = jnp.concatenate(b_parts, axis=0)
    b_tot = jnp.concatenate(tot_parts, axis=0)
    qg = (q * jnp.exp(b)).astype(BF16)
    kg = (k * jnp.exp(-b)).astype(BF16)
    k_end = (k * jnp.exp(b_tot - b)).astype(BF16)
    yield
    a = [jnp.where(keep, _dot_nt(chunk(qg, i), chunk(kg, i)), 0.0).astype(BF16) for i in range(nc)]
    dec = [jnp.broadcast_to(jnp.exp(tot_rows[i]), (GLA_DK, GLA_DK)).T for i in range(nc)]
    yield
    st_in = [None] * nc
    for i in (range(nc) if d == 0 else range(nc - 1, -1, -1)):
        st_in[i] = st.astype(BF16)
        u = _dot_tn(chunk(k_end, i), chunk(v, i))
        st = jnp.concatenate([dec[i] * st[:, j:j + GLA_DK] for j in range(0, GLA_DV, GLA_DK)], axis=1) + u
    yield
    o = [_dot(jnp.concatenate([chunk(qg, i), a[i]], axis=1),
              jnp.concatenate([st_in[i], chunk(v, i)], axis=0)) for i in range(nc)]
    return jnp.concatenate(o, axis=0), st


def _run_interleaved(chains):
    results = [None] * len(chains)
    live = list(enumerate(chains))
    while live:
        still = []
        for idx, gen in live:
            try:
                next(gen)
                still.append((idx, gen))
            except StopIteration as stop:
                results[idx] = stop.value
        live = still
    return results


def _gla_kernel(q_ref, k_ref, v_ref, r_ref, z_ref, wf_ref, wb_ref, bf_ref, bb_ref, gn_ref,
                o_ref, gf_s, gb_s, st_s, oacc_s):
    seq = q_ref.shape[0]
    c = GLA_CHUNK
    br = GLA_BLOCK * c
    n_blocks = seq // br

    z = z_ref[...].astype(BF16)
    gf_s[...] = _log_sigmoid(_dot(z, wf_ref[...]) + bf_ref[...]) * (1.0 / GLA_TAU)
    gb_s[...] = _log_sigmoid(_dot(z, wb_ref[...]) + bb_ref[...]) * (1.0 / GLA_TAU)
    st_s[...] = jnp.zeros_like(st_s)
    oacc_s[...] = jnp.zeros_like(oacc_s)

    row = lax.broadcasted_iota(jnp.int32, (c, c), 0)
    col = lax.broadcasted_iota(jnp.int32, (c, c), 1)
    tri = (jnp.where(col <= row, 1.0, 0.0).astype(BF16), jnp.where(col >= row, 1.0, 0.0).astype(BF16))
    keep = (col <= row, col > row)
    g_s = (gf_s, gb_s)

    def step(n, carry):
        where, chains = [], []
        for d in range(2):
            block = n if d == 0 else n_blocks - 1 - n
            rows = pl.ds(pl.multiple_of(block * br, br), br)
            for h in range(GLA_HG):
                ksl = slice(h * GLA_DK, (h + 1) * GLA_DK)
                vsl = slice(h * GLA_DV, (h + 1) * GLA_DV)
                where.append((rows, vsl, d * GLA_HG + h))
                chains.append(_gla_chain(d, g_s[d][rows, ksl], q_ref[rows, ksl] * (GLA_DK ** -0.5),
                                         k_ref[rows, ksl], v_ref[rows, vsl], st_s[d * GLA_HG + h],
                                         tri[d], keep[d]))
        for (rows, vsl, slot), (o, st) in zip(where, _run_interleaved(chains)):
            oacc_s[rows, vsl] += o
            st_s[slot] = st
        return carry

    lax.fori_loop(0, n_blocks, step, 0)

    for h in range(GLA_HG):
        vsl = slice(h * GLA_DV, (h + 1) * GLA_DV)
        o = _rms(oacc_s[:, vsl], gn_ref[...])
        r = r_ref[:, vsl]
        o_ref[:, vsl] = (o * (r * jax.nn.sigmoid(r))).astype(BF16)


def _gla(gqk, gv, gr, z, wdec, bdec, gnorm, seq):
    n = gqk.shape[0]
    nb = n // seq
    hg = GLA_HG
    groups = GLA_HEADS // hg
    return pl.pallas_call(
        _gla_kernel,
        grid=(nb, groups),
        in_specs=[
            pl.BlockSpec((seq, hg * GLA_DK), lambda b, g: (b, g)),
            pl.BlockSpec((seq, hg * GLA_DK), lambda b, g: (b, groups + g)),
            pl.BlockSpec((seq, hg * GLA_DV), lambda b, g: (b, g)),
            pl.BlockSpec((seq, hg * GLA_DV), lambda b, g: (b, g)),
            pl.BlockSpec((seq, LANE), lambda b, g: (b, 0)),
            pl.BlockSpec((LANE, hg * GLA_DK), lambda b, g: (0, g)),
            pl.BlockSpec((LANE, hg * GLA_DK), lambda b, g: (0, groups + g)),
            pl.BlockSpec((1, hg * GLA_DK), lambda b, g: (0, g)),
            pl.BlockSpec((1, hg * GLA_DK), lambda b, g: (0, groups + g)),
            pl.BlockSpec((1, GLA_DV), lambda b, g: (0, 0)),
        ],
        out_specs=pl.BlockSpec((seq, hg * GLA_DV), lambda b, g: (b, g)),
        out_shape=jax.ShapeDtypeStruct((n, GLA_V), BF16),
        scratch_shapes=[
            pltpu.VMEM((seq, hg * GLA_DK), F32),
            pltpu.VMEM((seq, hg * GLA_DK), F32),
            pltpu.VMEM((2 * hg, GLA_DK, GLA_DV), F32),
            pltpu.VMEM((seq, hg * GLA_DV), F32),
        ],
        compiler_params=_params("parallel", "parallel"),
        name="gla",
    )(gqk, gqk, gv, gr, z, wdec, wdec, bdec, bdec, gnorm)


def _attn_kernel(q_ref, k_ref, v_ref, o_ref, s0, s1, m0, m1, p0, p1, v1a, v1b):
    s_s, m_s, p_s, v1_s = (s0, s1), (m0, m1), (p0, p1), (v1a, v1b)
    hd = ATT_HD
    gw = ATT_GROUP * hd
    tq = Q_BLOCK
    nq = q_ref.shape[0] // tq
    for h in range(ATT_KV_HEADS):
        v1_s[h][:, :hd] = v_ref[:, h * hd:(h + 1) * hd]
        v1_s[h][:, hd:] = jnp.ones((v1_s[h].shape[0], hd), BF16)

    def rows_of(i):
        return pl.ds(pl.multiple_of(i * tq, tq), tq)

    def scores(h, i, slot):
        rows = rows_of(i)
        q = jnp.concatenate([q_ref[rows, h * gw + g * hd:h * gw + (g + 1) * hd] for g in range(ATT_GROUP)],
                            axis=0)
        s = _dot_nt(q, k_ref[:, h * hd:(h + 1) * hd])
        m_s[slot][...] = jnp.max(s, axis=-1, keepdims=True)
        s_s[slot][...] = s

    def softmax(slot):
        p_s[slot][...] = jnp.exp2(s_s[slot][...] - m_s[slot][...]).astype(BF16)

    def values(h, i, slot):
        rows = rows_of(i)
        ol = _dot(p_s[slot][...], v1_s[h][...])
        o = ol[:, :hd] / ol[:, hd:]
        for g in range(ATT_GROUP):
            o_ref[rows, h * gw + g * hd:h * gw + (g + 1) * hd] = o[g * tq:(g + 1) * tq].astype(BF16)

    scores(0, 0, 0)
    softmax(0)
    scores(0, 1, 1)
    for h in range(ATT_KV_HEADS):
        def pair(j, carry, h=h):
            values(h, 2 * j, 0)
            softmax(1)
            scores(h, 2 * j + 2, 0)
            values(h, 2 * j + 1, 1)
            softmax(0)
            scores(h, 2 * j + 3, 1)
            return carry

        lax.fori_loop(0, (nq - 2) // 2, pair, 0)
        more = h + 1 < ATT_KV_HEADS
        values(h, nq - 2, 0)
        softmax(1)
        if more:
            scores(h + 1, 0, 0)
        values(h, nq - 1, 1)
        if more:
            softmax(0)
            scores(h + 1, 1, 1)


def _attn(aq, ak, av, seq):
    n = aq.shape[0]
    nb = n // seq
    assert (seq // Q_BLOCK) % 2 == 0
    rows = ATT_GROUP * Q_BLOCK
    return pl.pallas_call(
        _attn_kernel,
        grid=(nb,),
        in_specs=[
            pl.BlockSpec((seq, ATT_Q), lambda b: (b, 0)),
            pl.BlockSpec((seq, ATT_KV), lambda b: (b, 0)),
            pl.BlockSpec((seq, ATT_KV), lambda b: (b, 0)),
        ],
        out_specs=pl.BlockSpec((seq, ATT_Q), lambda b: (b, 0)),
        out_shape=jax.ShapeDtypeStruct((n, ATT_Q), BF16),
        scratch_shapes=[
            pltpu.VMEM((rows, seq), F32), pltpu.VMEM((rows, seq), F32),
            pltpu.VMEM((rows, 1), F32), pltpu.VMEM((rows, 1), F32),
            pltpu.VMEM((rows, seq), BF16), pltpu.VMEM((rows, seq), BF16),
            pltpu.VMEM((seq, 2 * ATT_HD), BF16), pltpu.VMEM((seq, 2 * ATT_HD), BF16),
        ],
        compiler_params=_params("parallel"),
        name="attn",
    )(aq, ak, av)


def _mix_kernel(tiles_a, xa_ref, xb_ref, og_ref, oa_ref, gate_ref, wbg_ref, wba_ref, wout_ref,
                lnpost_ref, lnx_ref, wq_ref, x1_ref, qx_ref):
    out_a = _dot(og_ref[...], wbg_ref[...])
    out_b = _dot(oa_ref[...], wba_ref[...])
    mixed = (jax.nn.sigmoid(gate_ref[:, :D_MODEL]) * out_a
             + jax.nn.sigmoid(gate_ref[:, D_MODEL:]) * out_b)
    y = _dot(mixed.astype(BF16), wout_ref[...])
    x1 = _group_read(tiles_a, xa_ref, xb_ref) + _rms(y, lnpost_ref[...])
    x1_ref[...] = x1
    h = _rms(x1, lnx_ref[...]).astype(BF16)
    qx_ref[...] = (_dot(h, wq_ref[...]) * (X_HD ** -0.5)).astype(BF16)


def _mix(xa, xb, og, oa, gate, wbg, wba, wout, lnpost, lnx, wq):
    n = xa.shape[0] + xb.shape[0]
    tm = TM_MIX
    tiles_a = xa.shape[0] // tm
    row = lambda i: (i, 0)
    const = lambda i: (0, 0)
    sq = pl.BlockSpec((D_MODEL, D_MODEL), const)
    vec = pl.BlockSpec((1, D_MODEL), const)
    return pl.pallas_call(
        functools.partial(_mix_kernel, tiles_a),
        grid=(n // tm,),
        in_specs=[
            *_group_specs(tm, D_MODEL, tiles_a), pl.BlockSpec((tm, GLA_V), row),
            pl.BlockSpec((tm, ATT_Q), row), pl.BlockSpec((tm, 2 * D_MODEL), row),
            sq, sq, sq, vec, vec, sq,
        ],
        out_specs=[pl.BlockSpec((tm, D_MODEL), row), pl.BlockSpec((tm, D_MODEL), row)],
        out_shape=[jax.ShapeDtypeStruct((n, D_MODEL), F32), jax.ShapeDtypeStruct((n, D_MODEL), BF16)],
        compiler_params=_params("parallel"),
        name="mix",
    )(xa, xb, og, oa, gate, wbg, wba, wout, lnpost, lnx, wq)


def _memkv_kernel(tiles_a, ma_ref, mb_ref, ln_ref, w_ref, kv_ref):
    m = _rms(_group_read(tiles_a, ma_ref, mb_ref), ln_ref[...]).astype(BF16)
    kv_ref[...] = _dot(m, w_ref[...]).astype(BF16)


def _memkv(mem_a, mem_b, ln, wkv):
    n = mem_a.shape[0] + mem_b.shape[0]
    tm = 256
    tiles_a = mem_a.shape[0] // tm
    return pl.pallas_call(
        functools.partial(_memkv_kernel, tiles_a),
        grid=(n // tm,),
        in_specs=[
            *_group_specs(tm, D_MODEL, tiles_a),
            pl.BlockSpec((1, D_MODEL), lambda i: (0, 0)),
            pl.BlockSpec((D_MODEL, 2 * D_MODEL), lambda i: (0, 0)),
        ],
        out_specs=pl.BlockSpec((tm, 2 * D_MODEL), lambda i: (i, 0)),
        out_shape=jax.ShapeDtypeStruct((n, 2 * D_MODEL), BF16),
        compiler_params=_params("parallel"),
        name="memkv",
    )(mem_a, mem_b, ln, wkv)


def _xffn_kernel(tiles_a, x1_ref, qx_ref, kv_ref, wo_ref, lnxpost_ref, lnffn_ref, wi_ref, wo2_ref,
                 lnpost_ref, ya_ref, yb_ref):
    heads = []
    for j in range(X_HEADS):
        ksl = slice(j * X_HD, (j + 1) * X_HD)
        vsl = slice(D_MODEL + j * X_HD, D_MODEL + (j + 1) * X_HD)
        s = _dot_nt(qx_ref[:, ksl], kv_ref[:, ksl])
        p = jnp.exp(s - jnp.max(s, axis=-1, keepdims=True))
        l = jnp.sum(p, axis=-1, keepdims=True)
        heads.append((_dot(p.astype(BF16), kv_ref[:, vsl]) / l).astype(BF16))
    o = jnp.concatenate(heads, axis=1)
    x2 = x1_ref[...] + _rms(_dot(o, wo_ref[...]), lnxpost_ref[...])
    h = _rms(x2, lnffn_ref[...]).astype(BF16)
    g = _dot(h, wi_ref[:, :D_FF])
    u = _dot(h, wi_ref[:, D_FF:])
    f = _dot((g * jax.nn.sigmoid(g) * u).astype(BF16), wo2_ref[...])
    y = x2 + _rms(f, lnpost_ref[...])
    i = pl.program_id(0)

    @pl.when(i < tiles_a)
    def _():
        ya_ref[...] = y

    @pl.when(i >= tiles_a)
    def _():
        yb_ref[...] = y


def _xffn(x1, qx, kv, wo, lnxpost, lnffn, wi, wo2, lnpost, seq, n_mem, n_a):
    n = x1.shape[0]
    tm = TM_X
    tiles_per_seq = seq // tm
    tiles_a = n_a // tm
    row = lambda i: (i, 0)
    const = lambda i: (0, 0)
    resident = lambda shape: pl.BlockSpec(shape, const, pipeline_mode=pl.Buffered(1))
    vec = pl.BlockSpec((1, D_MODEL), const)
    return pl.pallas_call(
        functools.partial(_xffn_kernel, tiles_a),
        grid=(n // tm,),
        in_specs=[
            pl.BlockSpec((tm, D_MODEL), row), pl.BlockSpec((tm, D_MODEL), row),
            pl.BlockSpec((n_mem, 2 * D_MODEL), lambda i: (i // tiles_per_seq, 0)),
            resident((D_MODEL, D_MODEL)), vec, vec,
            resident((D_MODEL, 2 * D_FF)), resident((D_FF, D_MODEL)), vec,
        ],
        out_specs=list(_group_specs(tm, D_MODEL, tiles_a)),
        out_shape=[jax.ShapeDtypeStruct((n_a, D_MODEL), F32), jax.ShapeDtypeStruct((n - n_a, D_MODEL), F32)],
        compiler_params=_params("arbitrary"),
        name="xffn",
    )(x1, qx, kv, wo, lnxpost, lnffn, wi, wo2, lnpost)


def _rope_tables(seq):
    rows = seq // GRID_W
    row = jnp.broadcast_to(jnp.arange(rows)[:, None], (rows, GRID_W)).reshape(-1).astype(F32)
    col = jnp.broadcast_to(jnp.arange(GRID_W)[None, :], (rows, GRID_W)).reshape(-1).astype(F32)
    half = ATT_HD // 2
    inv_freq = ROPE_THETA ** (-jnp.arange(0, half, 2, dtype=F32) / half)
    ang_r = row[:, None] * inv_freq[None, :]
    ang_c = col[:, None] * inv_freq[None, :]
    cr, sr, cc, sc = jnp.cos(ang_r), jnp.sin(ang_r), jnp.cos(ang_c), jnp.sin(ang_c)
    return (jnp.concatenate([cr, cr, cc, cc], axis=-1),
            jnp.concatenate([-sr, sr, -sc, sc], axis=-1))


def _pack_w_in(w_in):
    o = [0]
    for s in (GLA_QK, GLA_QK, GLA_V, GLA_V, GLA_RANK, GLA_RANK, ATT_Q, ATT_KV, ATT_KV, D_MODEL, D_MODEL):
        o.append(o[-1] + s)
    w = w_in.astype(BF16)
    gq, gk, gv, gr, zf, zb, aq, ak, av, ga, gb = [w[:, o[i]:o[i + 1]] for i in range(11)]
    pad = jnp.zeros((D_MODEL, LANE - 2 * GLA_RANK), BF16)
    return jnp.concatenate([gq, gk, gv, gr, aq, ak, av, ga, gb, zf, zb, pad], axis=1)


def _pack_decay(wa_f, ba_f, wa_b, ba_b):
    w = jnp.zeros((LANE, 2 * GLA_QK), F32)
    w = w.at[:GLA_RANK, :GLA_QK].set(wa_f).at[GLA_RANK:2 * GLA_RANK, GLA_QK:].set(wa_b)
    return w.astype(BF16), jnp.concatenate([ba_f, ba_b])[None, :]


def _layer(xa, xb, mem_a, mem_b, seq, n_mem, ln_mix_pre, w_in, gla_wa_f, gla_ba_f, gla_wa_b, gla_ba_b, gla_norm,
           att_q_norm, att_k_norm, w_branch_gla, w_branch_att, w_out, ln_mix_post,
           ln_x_pre, ln_mem, x_wq, x_wkv, x_wo, ln_x_post, ln_ffn_pre, ffn_wi, ffn_wo, ln_ffn_post):
    vec = lambda v: v[None, :]
    cos, sin = _rope_tables(seq)
    wdec, bdec = _pack_decay(gla_wa_f, gla_ba_f, gla_wa_b, gla_ba_b)
    gqk, gv, gr, aq, ak, av, gate, z = _inproj(
        xa, xb, vec(ln_mix_pre), _pack_w_in(w_in), vec(att_q_norm), vec(att_k_norm), cos, sin, seq)
    og = _gla(gqk, gv, gr, z, wdec, bdec, vec(gla_norm), seq)
    oa = _attn(aq, ak, av, seq)
    x1, qx = _mix(xa, xb, og, oa, gate, w_branch_gla.astype(BF16), w_branch_att.astype(BF16),
                  w_out.astype(BF16), vec(ln_mix_post), vec(ln_x_pre), x_wq.astype(BF16))
    kv = _memkv(mem_a, mem_b, vec(ln_mem), x_wkv.astype(BF16))
    return _xffn(x1, qx, kv, x_wo.astype(BF16), vec(ln_x_post), vec(ln_ffn_pre), ffn_wi.astype(BF16),
                 ffn_wo.astype(BF16), vec(ln_ffn_post), seq, n_mem, xa.shape[0])


def kernel(x_prompt, x_sample, mem_prompt, mem_sample, ln_mix_pre, w_in, gla_wa_f, gla_ba_f, gla_wa_b, gla_ba_b, gla_norm, att_q_norm, att_k_norm, w_branch_gla, w_branch_att, w_out, ln_mix_post, ln_x_pre, ln_mem, x_wq, x_wkv, x_wo, ln_x_post, ln_ffn_pre, ffn_wi, ffn_wo, ln_ffn_post):
    params = (ln_mix_pre, w_in, gla_wa_f, gla_ba_f, gla_wa_b, gla_ba_b, gla_norm,
              att_q_norm, att_k_norm, w_branch_gla, w_branch_att, w_out, ln_mix_post,
              ln_x_pre, ln_mem, x_wq, x_wkv, x_wo, ln_x_post,
              ln_ffn_pre, ffn_wi, ffn_wo, ln_ffn_post)
    bp, seq, d = x_prompt.shape
    bs = x_sample.shape[0]
    n_mem = mem_prompt.shape[1]
    assert x_sample.shape[1:] == (seq, d) and mem_sample.shape[1:] == (n_mem, d) and d == D_MODEL
    xa = x_prompt.reshape(bp * seq, d)
    xb = x_sample.reshape(bs * seq, d)
    mem_a = mem_prompt.reshape(bp * n_mem, d)
    mem_b = mem_sample.reshape(bs * n_mem, d)
    for l in range(ln_mix_pre.shape[0]):
        xa, xb = _layer(xa, xb, mem_a, mem_b, seq, n_mem, *[p[l] for p in params])
    return (xa.reshape(bp, seq, d), xb.reshape(bs, seq, d))
```

```python
import functools

import jax
import jax.numpy as jnp
from jax import lax
from jax.experimental import pallas as pl
from jax.experimental.pallas import tpu as pltpu

F32 = jnp.float32
BF16 = jnp.bfloat16

D_MODEL = 1024
GRID_W = 64
EPS = 1e-6
GLA_HEADS = 4
GLA_DK = 128
GLA_DV = 256
GLA_QK = GLA_HEADS * GLA_DK
GLA_V = GLA_HEADS * GLA_DV
GLA_RANK = 16
GLA_TAU = 16.0
GLA_CHUNK = 64
ATT_HEADS = 8
ATT_KV_HEADS = 2
ATT_GROUP = ATT_HEADS // ATT_KV_HEADS
ATT_HD = 128
ATT_Q = ATT_HEADS * ATT_HD
ATT_KV = ATT_KV_HEADS * ATT_HD
ROPE_THETA = 10000.0
Q_BLOCK = 128
X_HEADS = 4
X_HD = D_MODEL // X_HEADS
D_FF = 2816

LOG2E = 1.4426950408889634
LANE = 128
VMEM_LIMIT = 56 * 1024 * 1024

C_GQK = 0
C_GV = C_GQK + 2 * GLA_QK
C_GR = C_GV + GLA_V
C_AQ = C_GR + GLA_V
C_AK = C_AQ + ATT_Q
C_AV = C_AK + ATT_KV
C_GATE = C_AV + ATT_KV
C_Z = C_GATE + 2 * D_MODEL
C_END = C_Z + LANE

TM_IN = 256
TM_MIX = 512
TM_X = 512
GLA_HG = 2
GLA_BLOCK = 8


def _rms(x, w):
    return x * lax.rsqrt(jnp.mean(x * x, axis=-1, keepdims=True) + EPS) * w


def _dot(a, b):
    return jnp.dot(a, b, preferred_element_type=F32)


def _dot_nt(a, b):
    return lax.dot_general(a, b, (((1,), (1,)), ((), ())), preferred_element_type=F32)


def _dot_tn(a, b):
    return lax.dot_general(a, b, (((0,), (0,)), ((), ())), preferred_element_type=F32)


def _params(*sem):
    return pltpu.CompilerParams(dimension_semantics=sem, vmem_limit_bytes=VMEM_LIMIT)


def _group_specs(tm, width, tiles_a):
    return (pl.BlockSpec((tm, width), lambda i: (jnp.minimum(i, tiles_a - 1), 0)),
            pl.BlockSpec((tm, width), lambda i: (jnp.maximum(i - tiles_a, 0), 0)))


def _group_read(tiles_a, a_ref, b_ref):
    return jnp.where(pl.program_id(0) < tiles_a, a_ref[...], b_ref[...])


def _inproj_kernel(tiles_a, xa_ref, xb_ref, ln_ref, w_ref, qn_ref, kn_ref, cos_ref, sin_ref,
                   gqk_ref, gv_ref, gr_ref, aq_ref, ak_ref, av_ref, gate_ref, z_ref):
    h = _rms(_group_read(tiles_a, xa_ref, xb_ref), ln_ref[...]).astype(BF16)

    def proj(lo, hi):
        return _dot(h, w_ref[:, lo:hi])

    gqk_ref[...] = proj(C_GQK, C_GV)
    gv_ref[...] = proj(C_GV, C_GR).astype(BF16)
    gr_ref[...] = proj(C_GR, C_AQ)

    cos = cos_ref[...]
    sin = sin_ref[...]
    lane = lax.broadcasted_iota(jnp.int32, cos.shape, 1)
    first_half = (lane % 64) < 32

    def norm_rope(y, gain, scale):
        yn = _rms(y, gain)
        partner = jnp.where(first_half, pltpu.roll(yn, LANE - 32, 1), pltpu.roll(yn, 32, 1))
        return ((yn * cos + partner * sin) * scale).astype(BF16)

    aq = proj(C_AQ, C_AK)
    for j in range(ATT_HEADS):
        sl = slice(j * ATT_HD, (j + 1) * ATT_HD)
        aq_ref[:, sl] = norm_rope(aq[:, sl], qn_ref[...], ATT_HD ** -0.5 * LOG2E)
    ak = proj(C_AK, C_AV)
    for j in range(ATT_KV_HEADS):
        sl = slice(j * ATT_HD, (j + 1) * ATT_HD)
        ak_ref[:, sl] = norm_rope(ak[:, sl], kn_ref[...], 1.0)
    av_ref[...] = proj(C_AV, C_GATE).astype(BF16)
    gate_ref[...] = proj(C_GATE, C_Z)
    z_ref[...] = proj(C_Z, C_END)


def _inproj(xa, xb, ln, w, qn, kn, cos, sin, seq):
    n = xa.shape[0] + xb.shape[0]
    tm = TM_IN
    tiles_a = xa.shape[0] // tm
    tiles_per_seq = seq // tm
    row = lambda i: (i, 0)
    const = lambda i: (0, 0)
    pos = lambda i: (i % tiles_per_seq, 0)
    outs = [
        (2 * GLA_QK, F32), (GLA_V, BF16), (GLA_V, F32), (ATT_Q, BF16),
        (ATT_KV, BF16), (ATT_KV, BF16), (2 * D_MODEL, F32), (LANE, F32),
    ]
    return pl.pallas_call(
        functools.partial(_inproj_kernel, tiles_a),
        grid=(n // tm,),
        in_specs=[
            *_group_specs(tm, D_MODEL, tiles_a),
            pl.BlockSpec((1, D_MODEL), const),
            pl.BlockSpec((D_MODEL, C_END), const, pipeline_mode=pl.Buffered(1)),
            pl.BlockSpec((1, ATT_HD), const),
            pl.BlockSpec((1, ATT_HD), const),
            pl.BlockSpec((tm, ATT_HD), pos),
            pl.BlockSpec((tm, ATT_HD), pos),
        ],
        out_specs=[pl.BlockSpec((tm, w_), row) for w_, _ in outs],
        out_shape=[jax.ShapeDtypeStruct((n, w_), dt) for w_, dt in outs],
        compiler_params=_params("parallel"),
        name="inproj",
    )(xa, xb, ln, w, qn, kn, cos, sin)


def _log_sigmoid(u):
    return jnp.minimum(u, 0.0) - jnp.log(1.0 + jnp.exp(-jnp.abs(u)))


def _gla_chain(d, g, q, k, v, st, tri, keep):
    c = GLA_CHUNK
    nc = g.shape[0] // c
    chunk = lambda t, i: t[i * c:(i + 1) * c]
    g_hi = g.astype(BF16)
    g_lo = (g - g_hi.astype(F32)).astype(BF16)
    g_hl = jnp.concatenate([g_hi, g_lo], axis=1)
    b_parts, tot_parts, tot_rows = [], [], []
    for i in range(nc):
        b2 = _dot(tri, chunk(g_hl, i))
        b_i = b2[:, :GLA_DK] + b2[:, GLA_DK:]
        t_i = b_i[c - 1:c] if d == 0 else b_i[0:1]
        b_parts.append(b_i)
        tot_rows.append(t_i)
        tot_parts.append(jnp.broadcast_to(t_i, (c, GLA_DK)))
    yield
    b = jnp.concatenate(b_parts, axis=0)
    b_tot = jnp.concatenate(tot_parts, axis=0)
    qg = (q * jnp.exp(b)).astype(BF16)
    kg = (k * jnp.exp(-b)).astype(BF16)
    k_end = (k * jnp.exp(b_tot - b)).astype(BF16)
    yield
    a = [jnp.where(keep, _dot_nt(chunk(qg, i), chunk(kg, i)), 0.0).astype(BF16) for i in range(nc)]
    dec = [jnp.broadcast_to(jnp.exp(tot_rows[i]), (GLA_DK, GLA_DK)).T for i in range(nc)]
    yield
    st_in = [None] * nc
    for i in (range(nc) if d == 0 else range(nc - 1, -1, -1)):
        st_in[i] = st.astype(BF16)
        u = _dot_tn(chunk(k_end, i), chunk(v, i))
        st = jnp.concatenate([dec[i] * st[:, j:j + GLA_DK] for j in range(0, GLA_DV, GLA_DK)], axis=1) + u
    yield
    o = [_dot(jnp.concatenate([chunk(qg, i), a[i]], axis=1),
              jnp.concatenate([st_in[i], chunk(v, i)], axis=0)) for i in range(nc)]
    return jnp.concatenate(o, axis=0), st


def _run_interleaved(chains):
    results = [None] * len(chains)
    live = list(enumerate(chains))
    while live:
        still = []
        for idx, gen in live:
            try:
                next(gen)
                still.append((idx, gen))
            except StopIteration as stop:
                results[idx] = stop.value
        live = still
    return results


def _gla_kernel(q_ref, k_ref, v_ref, r_ref, z_ref, wf_ref, wb_ref, bf_ref, bb_ref, gn_ref,
                o_ref, gf_s, gb_s, st_s, oacc_s):
    seq = q_ref.shape[0]
    c = GLA_CHUNK
    br = GLA_BLOCK * c
    n_blocks = seq // br

    z = z_ref[...].astype(BF16)
    gf_s[...] = _log_sigmoid(_dot(z, wf_ref[...]) + bf_ref[...]) * (1.0 / GLA_TAU)
    gb_s[...] = _log_sigmoid(_dot(z, wb_ref[...]) + bb_ref[...]) * (1.0 / GLA_TAU)
    st_s[...] = jnp.zeros_like(st_s)
    oacc_s[...] = jnp.zeros_like(oacc_s)

    row = lax.broadcasted_iota(jnp.int32, (c, c), 0)
    col = lax.broadcasted_iota(jnp.int32, (c, c), 1)
    tri = (jnp.where(col <= row, 1.0, 0.0).astype(BF16), jnp.where(col >= row, 1.0, 0.0).astype(BF16))
    keep = (col <= row, col > row)
    g_s = (gf_s, gb_s)

    def step(n, carry):
        where, chains = [], []
        for d in range(2):
            block = n if d == 0 else n_blocks - 1 - n
            rows = pl.ds(pl.multiple_of(block * br, br), br)
            for h in range(GLA_HG):
                ksl = slice(h * GLA_DK, (h + 1) * GLA_DK)
                vsl = slice(h * GLA_DV, (h + 1) * GLA_DV)
                where.append((rows, vsl, d * GLA_HG + h))
                chains.append(_gla_chain(d, g_s[d][rows, ksl], q_ref[rows, ksl] * (GLA_DK ** -0.5),
                                         k_ref[rows, ksl], v_ref[rows, vsl], st_s[d * GLA_HG + h],
                                         tri[d], keep[d]))
        for (rows, vsl, slot), (o, st) in zip(where, _run_interleaved(chains)):
            oacc_s[rows, vsl] += o
            st_s[slot] = st
        return carry

    lax.fori_loop(0, n_blocks, step, 0)

    for h in range(GLA_HG):
        vsl = slice(h * GLA_DV, (h + 1) * GLA_DV)
        o = _rms(oacc_s[:, vsl], gn_ref[...])
        r = r_ref[:, vsl]
        o_ref[:, vsl] = (o * (r * jax.nn.sigmoid(r))).astype(BF16)


def _gla(gqk, gv, gr, z, wdec, bdec, gnorm, seq):
    n = gqk.shape[0]
    nb = n // seq
    hg = GLA_HG
    groups = GLA_HEADS // hg
    return pl.pallas_call(
        _gla_kernel,
        grid=(nb, groups),
        in_specs=[
            pl.BlockSpec((seq, hg * GLA_DK), lambda b, g: (b, g)),
            pl.BlockSpec((seq, hg * GLA_DK), lambda b, g: (b, groups + g)),
            pl.BlockSpec((seq, hg * GLA_DV), lambda b, g: (b, g)),
            pl.BlockSpec((seq, hg * GLA_DV), lambda b, g: (b, g)),
            pl.BlockSpec((seq, LANE), lambda b, g: (b, 0)),
            pl.BlockSpec((LANE, hg * GLA_DK), lambda b, g: (0, g)),
            pl.BlockSpec((LANE, hg * GLA_DK), lambda b, g: (0, groups + g)),
            pl.BlockSpec((1, hg * GLA_DK), lambda b, g: (0, g)),
            pl.BlockSpec((1, hg * GLA_DK), lambda b, g: (0, groups + g)),
            pl.BlockSpec((1, GLA_DV), lambda b, g: (0, 0)),
        ],
        out_specs=pl.BlockSpec((seq, hg * GLA_DV), lambda b, g: (b, g)),
        out_shape=jax.ShapeDtypeStruct((n, GLA_V), BF16),
        scratch_shapes=[
            pltpu.VMEM((seq, hg * GLA_DK), F32),
            pltpu.VMEM((seq, hg * GLA_DK), F32),
            pltpu.VMEM((2 * hg, GLA_DK, GLA_DV), F32),
            pltpu.VMEM((seq, hg * GLA_DV), F32),
        ],
        compiler_params=_params("parallel", "parallel"),
        name="gla",
    )(gqk, gqk, gv, gr, z, wdec, wdec, bdec, bdec, gnorm)


def _attn_kernel(q_ref, k_ref, v_ref, o_ref, s0, s1, m0, m1, p0, p1, v1a, v1b):
    s_s, m_s, p_s, v1_s = (s0, s1), (m0, m1), (p0, p1), (v1a, v1b)
    hd = ATT_HD
    gw = ATT_GROUP * hd
    tq = Q_BLOCK
    nq = q_ref.shape[0] // tq
    for h in range(ATT_KV_HEADS):
        v1_s[h][:, :hd] = v_ref[:, h * hd:(h + 1) * hd]
        v1_s[h][:, hd:] = jnp.ones((v1_s[h].shape[0], hd), BF16)

    def rows_of(i):
        return pl.ds(pl.multiple_of(i * tq, tq), tq)

    def scores(h, i, slot):
        rows = rows_of(i)
        q = jnp.concatenate([q_ref[rows, h * gw + g * hd:h * gw + (g + 1) * hd] for g in range(ATT_GROUP)],
                            axis=0)
        s = _dot_nt(q, k_ref[:, h * hd:(h + 1) * hd])
        m_s[slot][...] = jnp.max(s, axis=-1, keepdims=True)
        s_s[slot][...] = s

    def softmax(slot):
        p_s[slot][...] = jnp.exp2(s_s[slot][...] - m_s[slot][...]).astype(BF16)

    def values(h, i, slot):
        rows = rows_of(i)
        ol = _dot(p_s[slot][...], v1_s[h][...])
        o = ol[:, :hd] / ol[:, hd:]
        for g in range(ATT_GROUP):
            o_ref[rows, h * gw + g * hd:h * gw + (g + 1) * hd] = o[g * tq:(g + 1) * tq].astype(BF16)

    scores(0, 0, 0)
    softmax(0)
    scores(0, 1, 1)
    for h in range(ATT_KV_HEADS):
        def pair(j, carry, h=h):
            values(h, 2 * j, 0)
            softmax(1)
            scores(h, 2 * j + 2, 0)
            values(h, 2 * j + 1, 1)
            softmax(0)
            scores(h, 2 * j + 3, 1)
            return carry

        lax.fori_loop(0, (nq - 2) // 2, pair, 0)
        more = h + 1 < ATT_KV_HEADS
        values(h, nq - 2, 0)
        softmax(1)
        if more:
            scores(h + 1, 0, 0)
        values(h, nq - 1, 1)
        if more:
            softmax(0)
            scores(h + 1, 1, 1)


def _attn(aq, ak, av, seq):
    n = aq.shape[0]
    nb = n // seq
    assert (seq // Q_BLOCK) % 2 == 0
    rows = ATT_GROUP * Q_BLOCK
    return pl.pallas_call(
        _attn_kernel,
        grid=(nb,),
        in_specs=[
            pl.BlockSpec((seq, ATT_Q), lambda b: (b, 0)),
            pl.BlockSpec((seq, ATT_KV), lambda b: (b, 0)),
            pl.BlockSpec((seq, ATT_KV), lambda b: (b, 0)),
        ],
        out_specs=pl.BlockSpec((seq, ATT_Q), lambda b: (b, 0)),
        out_shape=jax.ShapeDtypeStruct((n, ATT_Q), BF16),
        scratch_shapes=[
            pltpu.VMEM((rows, seq), F32), pltpu.VMEM((rows, seq), F32),
            pltpu.VMEM((rows, 1), F32), pltpu.VMEM((rows, 1), F32),
            pltpu.VMEM((rows, seq), BF16), pltpu.VMEM((rows, seq), BF16),
            pltpu.VMEM((seq, 2 * ATT_HD), BF16), pltpu.VMEM((seq, 2 * ATT_HD), BF16),
        ],
        compiler_params=_params("parallel"),
        name="attn",
    )(aq, ak, av)


def _mix_kernel(tiles_a, xa_ref, xb_ref, og_ref, oa_ref, gate_ref, wbg_ref, wba_ref, wout_ref,
                lnpost_ref, lnx_ref, wq_ref, x1_ref, qx_ref):
    out_a = _dot(og_ref[...], wbg_ref[...])
    out_b = _dot(oa_ref[...], wba_ref[...])
    mixed = (jax.nn.sigmoid(gate_ref[:, :D_MODEL]) * out_a
             + jax.nn.sigmoid(gate_ref[:, D_MODEL:]) * out_b)
    y = _dot(mixed.astype(BF16), wout_ref[...])
    x1 = _group_read(tiles_a, xa_ref, xb_ref) + _rms(y, lnpost_ref[...])
    x1_ref[...] = x1
    h = _rms(x1, lnx_ref[...]).astype(BF16)
    qx_ref[...] = (_dot(h, wq_ref[...]) * (X_HD ** -0.5)).astype(BF16)


def _mix(xa, xb, og, oa, gate, wbg, wba, wout, lnpost, lnx, wq):
    n = xa.shape[0] + xb.shape[0]
    tm = TM_MIX
    tiles_a = xa.shape[0] // tm
    row = lambda i: (i, 0)
    const = lambda i: (0, 0)
    sq = pl.BlockSpec((D_MODEL, D_MODEL), const)
    vec = pl.BlockSpec((1, D_MODEL), const)
    return pl.pallas_call(
        functools.partial(_mix_kernel, tiles_a),
        grid=(n // tm,),
        in_specs=[
            *_group_specs(tm, D_MODEL, tiles_a), pl.BlockSpec((tm, GLA_V), row),
            pl.BlockSpec((tm, ATT_Q), row), pl.BlockSpec((tm, 2 * D_MODEL), row),
            sq, sq, sq, vec, vec, sq,
        ],
        out_specs=[pl.BlockSpec((tm, D_MODEL), row), pl.BlockSpec((tm, D_MODEL), row)],
        out_shape=[jax.ShapeDtypeStruct((n, D_MODEL), F32), jax.ShapeDtypeStruct((n, D_MODEL), BF16)],
        compiler_params=_params("parallel"),
        name="mix",
    )(xa, xb, og, oa, gate, wbg, wba, wout, lnpost, lnx, wq)


def _memkv_kernel(tiles_a, ma_ref, mb_ref, ln_ref, w_ref, kv_ref):
    m = _rms(_group_read(tiles_a, ma_ref, mb_ref), ln_ref[...]).astype(BF16)
    kv_ref[...] = _dot(m, w_ref[...]).astype(BF16)


def _memkv(mem_a, mem_b, ln, wkv):
    n = mem_a.shape[0] + mem_b.shape[0]
    tm = 256
    tiles_a = mem_a.shape[0] // tm
    return pl.pallas_call(
        functools.partial(_memkv_kernel, tiles_a),
        grid=(n // tm,),
        in_specs=[
            *_group_specs(tm, D_MODEL, tiles_a),
            pl.BlockSpec((1, D_MODEL), lambda i: (0, 0)),
            pl.BlockSpec((D_MODEL, 2 * D_MODEL), lambda i: (0, 0)),
        ],
        out_specs=pl.BlockSpec((tm, 2 * D_MODEL), lambda i: (i, 0)),
        out_shape=jax.ShapeDtypeStruct((n, 2 * D_MODEL), BF16),
        compiler_params=_params("parallel"),
        name="memkv",
    )(mem_a, mem_b, ln, wkv)


def _xffn_kernel(tiles_a, x1_ref, qx_ref, kv_ref, wo_ref, lnxpost_ref, lnffn_ref, wi_ref, wo2_ref,
                 lnpost_ref, ya_ref, yb_ref):
    heads = []
    for j in range(X_HEADS):
        ksl = slice(j * X_HD, (j + 1) * X_HD)
        vsl = slice(D_MODEL + j * X_HD, D_MODEL + (j + 1) * X_HD)
        s = _dot_nt(qx_ref[:, ksl], kv_ref[:, ksl])
        p = jnp.exp(s - jnp.max(s, axis=-1, keepdims=True))
        l = jnp.sum(p, axis=-1, keepdims=True)
        heads.append((_dot(p.astype(BF16), kv_ref[:, vsl]) / l).astype(BF16))
    o = jnp.concatenate(heads, axis=1)
    x2 = x1_ref[...] + _rms(_dot(o, wo_ref[...]), lnxpost_ref[...])
    h = _rms(x2, lnffn_ref[...]).astype(BF16)
    g = _dot(h, wi_ref[:, :D_FF])
    u = _dot(h, wi_ref[:, D_FF:])
    f = _dot((g * jax.nn.sigmoid(g) * u).astype(BF16), wo2_ref[...])
    y = x2 + _rms(f, lnpost_ref[...])
    i = pl.program_id(0)

    @pl.when(i < tiles_a)
    def _():
        ya_ref[...] = y

    @pl.when(i >= tiles_a)
    def _():
        yb_ref[...] = y


def _xffn(x1, qx, kv, wo, lnxpost, lnffn, wi, wo2, lnpost, seq, n_mem, n_a):
    n = x1.shape[0]
    tm = TM_X
    tiles_per_seq = seq // tm
    tiles_a = n_a // tm
    row = lambda i: (i, 0)
    const = lambda i: (0, 0)
    resident = lambda shape: pl.BlockSpec(shape, const, pipeline_mode=pl.Buffered(1))
    vec = pl.BlockSpec((1, D_MODEL), const)
    return pl.pallas_call(
        functools.partial(_xffn_kernel, tiles_a),
        grid=(n // tm,),
        in_specs=[
            pl.BlockSpec((tm, D_MODEL), row), pl.BlockSpec((tm, D_MODEL), row),
            pl.BlockSpec((n_mem, 2 * D_MODEL), lambda i: (i // tiles_per_seq, 0)),
            resident((D_MODEL, D_MODEL)), vec, vec,
            resident((D_MODEL, 2 * D_FF)), resident((D_FF, D_MODEL)), vec,
        ],
        out_specs=list(_group_specs(tm, D_MODEL, tiles_a)),
        out_shape=[jax.ShapeDtypeStruct((n_a, D_MODEL), F32), jax.ShapeDtypeStruct((n - n_a, D_MODEL), F32)],
        compiler_params=_params("arbitrary"),
        name="xffn",
    )(x1, qx, kv, wo, lnxpost, lnffn, wi, wo2, lnpost)


def _rope_tables(seq):
    rows = seq // GRID_W
    row = jnp.broadcast_to(jnp.arange(rows)[:, None], (rows, GRID_W)).reshape(-1).astype(F32)
    col = jnp.broadcast_to(jnp.arange(GRID_W)[None, :], (rows, GRID_W)).reshape(-1).astype(F32)
    half = ATT_HD // 2
    inv_freq = ROPE_THETA ** (-jnp.arange(0, half, 2, dtype=F32) / half)
    ang_r = row[:, None] * inv_freq[None, :]
    ang_c = col[:, None] * inv_freq[None, :]
    cr, sr, cc, sc = jnp.cos(ang_r), jnp.sin(ang_r), jnp.cos(ang_c), jnp.sin(ang_c)
    return (jnp.concatenate([cr, cr, cc, cc], axis=-1),
            jnp.concatenate([-sr, sr, -sc, sc], axis=-1))


def _pack_w_in(w_in):
    z_lo = C_AQ
    z_hi = z_lo + 2 * GLA_RANK
    assert w_in.shape[1] - 2 * GLA_RANK == C_Z
    w = w_in.astype(BF16)
    pad = jnp.zeros((D_MODEL, LANE - 2 * GLA_RANK), BF16)
    return jnp.concatenate([w[:, :z_lo], w[:, z_hi:], w[:, z_lo:z_hi], pad], axis=1)


def _pack_decay(wa_f, ba_f, wa_b, ba_b):
    w = jnp.zeros((LANE, 2 * GLA_QK), F32)
    w = w.at[:GLA_RANK, :GLA_QK].set(wa_f).at[GLA_RANK:2 * GLA_RANK, GLA_QK:].set(wa_b)
    return w.astype(BF16), jnp.concatenate([ba_f, ba_b])[None, :]


def _layer(xa, xb, mem_a, mem_b, seq, n_mem, ln_mix_pre, w_in, gla_wa_f, gla_ba_f, gla_wa_b, gla_ba_b, gla_norm,
           att_q_norm, att_k_norm, w_branch_gla, w_branch_att, w_out, ln_mix_post,
           ln_x_pre, ln_mem, x_wq, x_wkv, x_wo, ln_x_post, ln_ffn_pre, ffn_wi, ffn_wo, ln_ffn_post):
    vec = lambda v: v[None, :]
    cos, sin = _rope_tables(seq)
    wdec, bdec = _pack_decay(gla_wa_f, gla_ba_f, gla_wa_b, gla_ba_b)
    gqk, gv, gr, aq, ak, av, gate, z = _inproj(
        xa, xb, vec(ln_mix_pre), _pack_w_in(w_in), vec(att_q_norm), vec(att_k_norm), cos, sin, seq)
    og = _gla(gqk, gv, gr, z, wdec, bdec, vec(gla_norm), seq)
    oa = _attn(aq, ak, av, seq)
    x1, qx = _mix(xa, xb, og, oa, gate, w_branch_gla.astype(BF16), w_branch_att.astype(BF16),
                  w_out.astype(BF16), vec(ln_mix_post), vec(ln_x_pre), x_wq.astype(BF16))
    kv = _memkv(mem_a, mem_b, vec(ln_mem), x_wkv.astype(BF16))
    return _xffn(x1, qx, kv, x_wo.astype(BF16), vec(ln_x_post), vec(ln_ffn_pre), ffn_wi.astype(BF16),
                 ffn_wo.astype(BF16), vec(ln_ffn_post), seq, n_mem, xa.shape[0])


def kernel(x_prompt, x_sample, mem_prompt, mem_sample, ln_mix_pre, w_in, gla_wa_f, gla_ba_f, gla_wa_b, gla_ba_b, gla_norm, att_q_norm, att_k_norm, w_branch_gla, w_branch_att, w_out, ln_mix_post, ln_x_pre, ln_mem, x_wq, x_wkv, x_wo, ln_x_post, ln_ffn_pre, ffn_wi, ffn_wo, ln_ffn_post):
    params = (ln_mix_pre, w_in, gla_wa_f, gla_ba_f, gla_wa_b, gla_ba_b, gla_norm,
              att_q_norm, att_k_norm, w_branch_gla, w_branch_att, w_out, ln_mix_post,
              ln_x_pre, ln_mem, x_wq, x_wkv, x_wo, ln_x_post,
              ln_ffn_pre, ffn_wi, ffn_wo, ln_ffn_post)
    bp, seq, d = x_prompt.shape
    bs = x_sample.shape[0]
    n_mem = mem_prompt.shape[1]
    assert x_sample.shape[1:] == (seq, d) and mem_sample.shape[1:] == (n_mem, d) and d == D_MODEL
    xa = x_prompt.reshape(bp * seq, d)
    xb = x_sample.reshape(bs * seq, d)
    mem_a = mem_prompt.reshape(bp * n_mem, d)
    mem_b = mem_sample.reshape(bs * n_mem, d)
    for l in range(ln_mix_pre.shape[0]):
        xa, xb = _layer(xa, xb, mem_a, mem_b, seq, n_mem, *[p[l] for p in params])
    return (xa.reshape(bp, seq, d), xb.reshape(bs, seq, d))
```

```python
import functools

import jax
import jax.numpy as jnp
from jax import lax
from jax.experimental import pallas as pl
from jax.experimental.pallas import tpu as pltpu

F32 = jnp.float32
BF16 = jnp.bfloat16

D_MODEL = 1024
GRID_W = 64
EPS = 1e-6
GLA_HEADS = 4
GLA_DK = 128
GLA_DV = 256
GLA_QK = GLA_HEADS * GLA_DK
GLA_V = GLA_HEADS * GLA_DV
GLA_RANK = 16
GLA_TAU = 16.0
GLA_CHUNK = 64
ATT_HEADS = 8
ATT_KV_HEADS = 2
ATT_GROUP = ATT_HEADS // ATT_KV_HEADS
ATT_HD = 128
ATT_Q = ATT_HEADS * ATT_HD
ATT_KV = ATT_KV_HEADS * ATT_HD
ROPE_THETA = 10000.0
Q_BLOCK = 128
X_HEADS = 4
X_HD = D_MODEL // X_HEADS
D_FF = 2816

LOG2E = 1.4426950408889634
LANE = 128
VMEM_LIMIT = 56 * 1024 * 1024

C_GQK = 0
C_GV = C_GQK + 2 * GLA_QK
C_GR = C_GV + GLA_V
C_AQ = C_GR + GLA_V
C_AK = C_AQ + ATT_Q
C_AV = C_AK + ATT_KV
C_GATE = C_AV + ATT_KV
C_Z = C_GATE + 2 * D_MODEL
C_END = C_Z + LANE

TM_IN = 256
TM_MIX = 512
TM_X = 512
GLA_HG = 2
GLA_BLOCK = 8


def _rms(x, w):
    return x * lax.rsqrt(jnp.mean(x * x, axis=-1, keepdims=True) + EPS) * w


def _dot(a, b):
    return jnp.dot(a, b, preferred_element_type=F32)


def _dot_nt(a, b):
    return lax.dot_general(a, b, (((1,), (1,)), ((), ())), preferred_element_type=F32)


def _dot_tn(a, b):
    return lax.dot_general(a, b, (((0,), (0,)), ((), ())), preferred_element_type=F32)


def _params(*sem):
    return pltpu.CompilerParams(dimension_semantics=sem, vmem_limit_bytes=VMEM_LIMIT)


def _group_specs(tm, width, tiles_a):
    return (pl.BlockSpec((tm, width), lambda i: (jnp.minimum(i, tiles_a - 1), 0)),
            pl.BlockSpec((tm, width), lambda i: (jnp.maximum(i - tiles_a, 0), 0)))


def _group_read(tiles_a, a_ref, b_ref):
    return jnp.where(pl.program_id(0) < tiles_a, a_ref[...], b_ref[...])


def _inproj_kernel(tiles_a, xa_ref, xb_ref, ln_ref, wa_ref, wb_ref, wz_ref, qn_ref, kn_ref, cos_ref, sin_ref,
                   gqk_ref, gv_ref, gr_ref, aq_ref, ak_ref, av_ref, gate_ref, z_ref):
    h = _rms(_group_read(tiles_a, xa_ref, xb_ref), ln_ref[...]).astype(BF16)

    def proj(lo, hi):
        if hi <= C_AQ:
            return _dot(h, wa_ref[:, lo:hi])
        if lo >= C_Z:
            return _dot(h, wz_ref[...])
        return _dot(h, wb_ref[:, lo - C_AQ:hi - C_AQ])

    gqk_ref[...] = proj(C_GQK, C_GV)
    gv_ref[...] = proj(C_GV, C_GR).astype(BF16)
    gr_ref[...] = proj(C_GR, C_AQ)

    cos = cos_ref[...]
    sin = sin_ref[...]
    lane = lax.broadcasted_iota(jnp.int32, cos.shape, 1)
    first_half = (lane % 64) < 32

    def norm_rope(y, gain, scale):
        yn = _rms(y, gain)
        partner = jnp.where(first_half, pltpu.roll(yn, LANE - 32, 1), pltpu.roll(yn, 32, 1))
        return ((yn * cos + partner * sin) * scale).astype(BF16)

    aq = proj(C_AQ, C_AK)
    for j in range(ATT_HEADS):
        sl = slice(j * ATT_HD, (j + 1) * ATT_HD)
        aq_ref[:, sl] = norm_rope(aq[:, sl], qn_ref[...], ATT_HD ** -0.5 * LOG2E)
    ak = proj(C_AK, C_AV)
    for j in range(ATT_KV_HEADS):
        sl = slice(j * ATT_HD, (j + 1) * ATT_HD)
        ak_ref[:, sl] = norm_rope(ak[:, sl], kn_ref[...], 1.0)
    av_ref[...] = proj(C_AV, C_GATE).astype(BF16)
    gate_ref[...] = proj(C_GATE, C_Z)
    z_ref[...] = proj(C_Z, C_END)


def _inproj(xa, xb, ln, w, qn, kn, cos, sin, seq):
    n = xa.shape[0] + xb.shape[0]
    tm = TM_IN
    tiles_a = xa.shape[0] // tm
    tiles_per_seq = seq // tm
    row = lambda i: (i, 0)
    const = lambda i: (0, 0)
    pos = lambda i: (i % tiles_per_seq, 0)
    outs = [
        (2 * GLA_QK, F32), (GLA_V, BF16), (GLA_V, F32), (ATT_Q, BF16),
        (ATT_KV, BF16), (ATT_KV, BF16), (2 * D_MODEL, F32), (LANE, F32),
    ]
    return pl.pallas_call(
        functools.partial(_inproj_kernel, tiles_a),
        grid=(n // tm,),
        in_specs=[
            *_group_specs(tm, D_MODEL, tiles_a),
            pl.BlockSpec((1, D_MODEL), const),
            pl.BlockSpec((D_MODEL, C_AQ), const, pipeline_mode=pl.Buffered(1)),
            pl.BlockSpec((D_MODEL, C_Z - C_AQ), const, pipeline_mode=pl.Buffered(1)),
            pl.BlockSpec((D_MODEL, LANE), const),
            pl.BlockSpec((1, ATT_HD), const),
            pl.BlockSpec((1, ATT_HD), const),
            pl.BlockSpec((tm, ATT_HD), pos),
            pl.BlockSpec((tm, ATT_HD), pos),
        ],
        out_specs=[pl.BlockSpec((tm, w_), row) for w_, _ in outs],
        out_shape=[jax.ShapeDtypeStruct((n, w_), dt) for w_, dt in outs],
        compiler_params=_params("parallel"),
        name="inproj",
    )(xa, xb, ln, *w, qn, kn, cos, sin)


def _log_sigmoid(u):
    return jnp.minimum(u, 0.0) - jnp.log(1.0 + jnp.exp(-jnp.abs(u)))


def _gla_chain(d, g, q, k, v, st, tri, keep):
    c = GLA_CHUNK
    nc = g.shape[0] // c
    chunk = lambda t, i: t[i * c:(i + 1) * c]
    g_hi = g.astype(BF16)
    g_lo = (g - g_hi.astype(F32)).astype(BF16)
    g_hl = jnp.concatenate([g_hi, g_lo], axis=1)
    b_parts, tot_parts, tot_rows = [], [], []
    for i in range(nc):
        b2 = _dot(tri, chunk(g_hl, i))
        b_i = b2[:, :GLA_DK] + b2[:, GLA_DK:]
        t_i = b_i[c - 1:c] if d == 0 else b_i[0:1]
        b_parts.append(b_i)
        tot_rows.append(t_i)
        tot_parts.append(jnp.broadcast_to(t_i, (c, GLA_DK)))
    yield
    b = jnp.concatenate(b_parts, axis=0)
    b_tot = jnp.concatenate(tot_parts, axis=0)
    qg = (q * jnp.exp(b)).astype(BF16)
    kg = (k * jnp.exp(-b)).astype(BF16)
    k_end = (k * jnp.exp(b_tot - b)).astype(BF16)
    yield
    a = [jnp.where(keep, _dot_nt(chunk(qg, i), chunk(kg, i)), 0.0).astype(BF16) for i in range(nc)]
    dec = [jnp.broadcast_to(jnp.exp(tot_rows[i]), (GLA_DK, GLA_DK)).T for i in range(nc)]
    yield
    st_in = [None] * nc
    for i in (range(nc) if d == 0 else range(nc - 1, -1, -1)):
        st_in[i] = st.astype(BF16)
        u = _dot_tn(chunk(k_end, i), chunk(v, i))
        st = jnp.concatenate([dec[i] * st[:, j:j + GLA_DK] for j in range(0, GLA_DV, GLA_DK)], axis=1) + u
    yield
    o = [_dot(jnp.concatenate([chunk(qg, i), a[i]], axis=1),
              jnp.concatenate([st_in[i], chunk(v, i)], axis=0)) for i in range(nc)]
    return jnp.concatenate(o, axis=0), st


def _run_interleaved(chains):
    results = [None] * len(chains)
    live = list(enumerate(chains))
    while live:
        still = []
        for idx, gen in live:
            try:
                next(gen)
                still.append((idx, gen))
            except StopIteration as stop:
                results[idx] = stop.value
        live = still
    return results


def _gla_kernel(q_ref, k_ref, v_ref, r_ref, z_ref, wf_ref, wb_ref, bf_ref, bb_ref, gn_ref,
                o_ref, gf_s, gb_s, st_s, oacc_s):
    seq = q_ref.shape[0]
    c = GLA_CHUNK
    br = GLA_BLOCK * c
    n_blocks = seq // br

    z = z_ref[...].astype(BF16)
    gf_s[...] = _log_sigmoid(_dot(z, wf_ref[...]) + bf_ref[...]) * (1.0 / GLA_TAU)
    gb_s[...] = _log_sigmoid(_dot(z, wb_ref[...]) + bb_ref[...]) * (1.0 / GLA_TAU)
    st_s[...] = jnp.zeros_like(st_s)
    oacc_s[...] = jnp.zeros_like(oacc_s)

    row = lax.broadcasted_iota(jnp.int32, (c, c), 0)
    col = lax.broadcasted_iota(jnp.int32, (c, c), 1)
    tri = (jnp.where(col <= row, 1.0, 0.0).astype(BF16), jnp.where(col >= row, 1.0, 0.0).astype(BF16))
    keep = (col <= row, col > row)
    g_s = (gf_s, gb_s)

    def step(n, carry):
        where, chains = [], []
        for d in range(2):
            block = n if d == 0 else n_blocks - 1 - n
            rows = pl.ds(pl.multiple_of(block * br, br), br)
            for h in range(GLA_HG):
                ksl = slice(h * GLA_DK, (h + 1) * GLA_DK)
                vsl = slice(h * GLA_DV, (h + 1) * GLA_DV)
                where.append((rows, vsl, d * GLA_HG + h))
                chains.append(_gla_chain(d, g_s[d][rows, ksl], q_ref[rows, ksl] * (GLA_DK ** -0.5),
                                         k_ref[rows, ksl], v_ref[rows, vsl], st_s[d * GLA_HG + h],
                                         tri[d], keep[d]))
        for (rows, vsl, slot), (o, st) in zip(where, _run_interleaved(chains)):
            oacc_s[rows, vsl] += o
            st_s[slot] = st
        return carry

    lax.fori_loop(0, n_blocks, step, 0)

    for h in range(GLA_HG):
        vsl = slice(h * GLA_DV, (h + 1) * GLA_DV)
        o = _rms(oacc_s[:, vsl], gn_ref[...])
        r = r_ref[:, vsl]
        o_ref[:, vsl] = (o * (r * jax.nn.sigmoid(r))).astype(BF16)


def _gla(gqk, gv, gr, z, wdec, bdec, gnorm, seq):
    n = gqk.shape[0]
    nb = n // seq
    hg = GLA_HG
    groups = GLA_HEADS // hg
    return pl.pallas_call(
        _gla_kernel,
        grid=(nb, groups),
        in_specs=[
            pl.BlockSpec((seq, hg * GLA_DK), lambda b, g: (b, g)),
            pl.BlockSpec((seq, hg * GLA_DK), lambda b, g: (b, groups + g)),
            pl.BlockSpec((seq, hg * GLA_DV), lambda b, g: (b, g)),
            pl.BlockSpec((seq, hg * GLA_DV), lambda b, g: (b, g)),
            pl.BlockSpec((seq, LANE), lambda b, g: (b, 0)),
            pl.BlockSpec((LANE, hg * GLA_DK), lambda b, g: (0, g)),
            pl.BlockSpec((LANE, hg * GLA_DK), lambda b, g: (0, groups + g)),
            pl.BlockSpec((1, hg * GLA_DK), lambda b, g: (0, g)),
            pl.BlockSpec((1, hg * GLA_DK), lambda b, g: (0, groups + g)),
            pl.BlockSpec((1, GLA_DV), lambda b, g: (0, 0)),
        ],
        out_specs=pl.BlockSpec((seq, hg * GLA_DV), lambda b, g: (b, g)),
        out_shape=jax.ShapeDtypeStruct((n, GLA_V), BF16),
        scratch_shapes=[
            pltpu.VMEM((seq, hg * GLA_DK), F32),
            pltpu.VMEM((seq, hg * GLA_DK), F32),
            pltpu.VMEM((2 * hg, GLA_DK, GLA_DV), F32),
            pltpu.VMEM((seq, hg * GLA_DV), F32),
        ],
        compiler_params=_params("parallel", "parallel"),
        name="gla",
    )(gqk, gqk, gv, gr, z, wdec, wdec, bdec, bdec, gnorm)


def _attn_kernel(q_ref, k_ref, v_ref, o_ref, s0, s1, m0, m1, p0, p1, v1a, v1b):
    s_s, m_s, p_s, v1_s = (s0, s1), (m0, m1), (p0, p1), (v1a, v1b)
    hd = ATT_HD
    gw = ATT_GROUP * hd
    tq = Q_BLOCK
    nq = q_ref.shape[0] // tq
    for h in range(ATT_KV_HEADS):
        v1_s[h][:, :hd] = v_ref[:, h * hd:(h + 1) * hd]
        v1_s[h][:, hd:] = jnp.ones((v1_s[h].shape[0], hd), BF16)

    def rows_of(i):
        return pl.ds(pl.multiple_of(i * tq, tq), tq)

    def scores(h, i, slot):
        rows = rows_of(i)
        q = jnp.concatenate([q_ref[rows, h * gw + g * hd:h * gw + (g + 1) * hd] for g in range(ATT_GROUP)],
                            axis=0)
        s = _dot_nt(q, k_ref[:, h * hd:(h + 1) * hd])
        m_s[slot][...] = jnp.max(s, axis=-1, keepdims=True)
        s_s[slot][...] = s

    def softmax(slot):
        p_s[slot][...] = jnp.exp2(s_s[slot][...] - m_s[slot][...]).astype(BF16)

    def values(h, i, slot):
        rows = rows_of(i)
        ol = _dot(p_s[slot][...], v1_s[h][...])
        o = ol[:, :hd] / ol[:, hd:]
        for g in range(ATT_GROUP):
            o_ref[rows, h * gw + g * hd:h * gw + (g + 1) * hd] = o[g * tq:(g + 1) * tq].astype(BF16)

    scores(0, 0, 0)
    softmax(0)
    scores(0, 1, 1)
    for h in range(ATT_KV_HEADS):
        def pair(j, carry, h=h):
            values(h, 2 * j, 0)
            softmax(1)
            scores(h, 2 * j + 2, 0)
            values(h, 2 * j + 1, 1)
            softmax(0)
            scores(h, 2 * j + 3, 1)
            return carry

        lax.fori_loop(0, (nq - 2) // 2, pair, 0)
        more = h + 1 < ATT_KV_HEADS
        values(h, nq - 2, 0)
        softmax(1)
        if more:
            scores(h + 1, 0, 0)
        values(h, nq - 1, 1)
        if more:
            softmax(0)
            scores(h + 1, 1, 1)


def _attn(aq, ak, av, seq):
    n = aq.shape[0]
    nb = n // seq
    assert (seq // Q_BLOCK) % 2 == 0
    rows = ATT_GROUP * Q_BLOCK
    return pl.pallas_call(
        _attn_kernel,
        grid=(nb,),
        in_specs=[
            pl.BlockSpec((seq, ATT_Q), lambda b: (b, 0)),
            pl.BlockSpec((seq, ATT_KV), lambda b: (b, 0)),
            pl.BlockSpec((seq, ATT_KV), lambda b: (b, 0)),
        ],
        out_specs=pl.BlockSpec((seq, ATT_Q), lambda b: (b, 0)),
        out_shape=jax.ShapeDtypeStruct((n, ATT_Q), BF16),
        scratch_shapes=[
            pltpu.VMEM((rows, seq), F32), pltpu.VMEM((rows, seq), F32),
            pltpu.VMEM((rows, 1), F32), pltpu.VMEM((rows, 1), F32),
            pltpu.VMEM((rows, seq), BF16), pltpu.VMEM((rows, seq), BF16),
            pltpu.VMEM((seq, 2 * ATT_HD), BF16), pltpu.VMEM((seq, 2 * ATT_HD), BF16),
        ],
        compiler_params=_params("parallel"),
        name="attn",
    )(aq, ak, av)


def _mix_kernel(tiles_a, xa_ref, xb_ref, og_ref, oa_ref, gate_ref, wbg_ref, wba_ref, wout_ref,
                lnpost_ref, lnx_ref, wq_ref, x1_ref, qx_ref):
    out_a = _dot(og_ref[...], wbg_ref[...])
    out_b = _dot(oa_ref[...], wba_ref[...])
    mixed = (jax.nn.sigmoid(gate_ref[:, :D_MODEL]) * out_a
             + jax.nn.sigmoid(gate_ref[:, D_MODEL:]) * out_b)
    y = _dot(mixed.astype(BF16), wout_ref[...])
    x1 = _group_read(tiles_a, xa_ref, xb_ref) + _rms(y, lnpost_ref[...])
    x1_ref[...] = x1
    h = _rms(x1, lnx_ref[...]).astype(BF16)
    qx_ref[...] = (_dot(h, wq_ref[...]) * (X_HD ** -0.5)).astype(BF16)


def _mix(xa, xb, og, oa, gate, wbg, wba, wout, lnpost, lnx, wq):
    n = xa.shape[0] + xb.shape[0]
    tm = TM_MIX
    tiles_a = xa.shape[0] // tm
    row = lambda i: (i, 0)
    const = lambda i: (0, 0)
    sq = pl.BlockSpec((D_MODEL, D_MODEL), const)
    vec = pl.BlockSpec((1, D_MODEL), const)
    return pl.pallas_call(
        functools.partial(_mix_kernel, tiles_a),
        grid=(n // tm,),
        in_specs=[
            *_group_specs(tm, D_MODEL, tiles_a), pl.BlockSpec((tm, GLA_V), row),
            pl.BlockSpec((tm, ATT_Q), row), pl.BlockSpec((tm, 2 * D_MODEL), row),
            sq, sq, sq, vec, vec, sq,
        ],
        out_specs=[pl.BlockSpec((tm, D_MODEL), row), pl.BlockSpec((tm, D_MODEL), row)],
        out_shape=[jax.ShapeDtypeStruct((n, D_MODEL), F32), jax.ShapeDtypeStruct((n, D_MODEL), BF16)],
        compiler_params=_params("parallel"),
        name="mix",
    )(xa, xb, og, oa, gate, wbg, wba, wout, lnpost, lnx, wq)


def _memkv_kernel(tiles_a, ma_ref, mb_ref, ln_ref, w_ref, kv_ref):
    m = _rms(_group_read(tiles_a, ma_ref, mb_ref), ln_ref[...]).astype(BF16)
    kv_ref[...] = _dot(m, w_ref[...]).astype(BF16)


def _memkv(mem_a, mem_b, ln, wkv):
    n = mem_a.shape[0] + mem_b.shape[0]
    tm = 256
    tiles_a = mem_a.shape[0] // tm
    return pl.pallas_call(
        functools.partial(_memkv_kernel, tiles_a),
        grid=(n // tm,),
        in_specs=[
            *_group_specs(tm, D_MODEL, tiles_a),
            pl.BlockSpec((1, D_MODEL), lambda i: (0, 0)),
            pl.BlockSpec((D_MODEL, 2 * D_MODEL), lambda i: (0, 0)),
        ],
        out_specs=pl.BlockSpec((tm, 2 * D_MODEL), lambda i: (i, 0)),
        out_shape=jax.ShapeDtypeStruct((n, 2 * D_MODEL), BF16),
        compiler_params=_params("parallel"),
        name="memkv",
    )(mem_a, mem_b, ln, wkv)


def _xffn_kernel(tiles_a, x1_ref, qx_ref, kv_ref, wo_ref, lnxpost_ref, lnffn_ref, wi_ref, wo2_ref,
                 lnpost_ref, ya_ref, yb_ref):
    heads = []
    for j in range(X_HEADS):
        ksl = slice(j * X_HD, (j + 1) * X_HD)
        vsl = slice(D_MODEL + j * X_HD, D_MODEL + (j + 1) * X_HD)
        s = _dot_nt(qx_ref[:, ksl], kv_ref[:, ksl])
        p = jnp.exp(s - jnp.max(s, axis=-1, keepdims=True))
        l = jnp.sum(p, axis=-1, keepdims=True)
        heads.append((_dot(p.astype(BF16), kv_ref[:, vsl]) / l).astype(BF16))
    o = jnp.concatenate(heads, axis=1)
    x2 = x1_ref[...] + _rms(_dot(o, wo_ref[...]), lnxpost_ref[...])
    h = _rms(x2, lnffn_ref[...]).astype(BF16)
    g = _dot(h, wi_ref[:, :D_FF])
    u = _dot(h, wi_ref[:, D_FF:])
    f = _dot((g * jax.nn.sigmoid(g) * u).astype(BF16), wo2_ref[...])
    y = x2 + _rms(f, lnpost_ref[...])
    i = pl.program_id(0)

    @pl.when(i < tiles_a)
    def _():
        ya_ref[...] = y

    @pl.when(i >= tiles_a)
    def _():
        yb_ref[...] = y


def _xffn(x1, qx, kv, wo, lnxpost, lnffn, wi, wo2, lnpost, seq, n_mem, n_a):
    n = x1.shape[0]
    tm = TM_X
    tiles_per_seq = seq // tm
    tiles_a = n_a // tm
    row = lambda i: (i, 0)
    const = lambda i: (0, 0)
    resident = lambda shape: pl.BlockSpec(shape, const, pipeline_mode=pl.Buffered(1))
    vec = pl.BlockSpec((1, D_MODEL), const)
    return pl.pallas_call(
        functools.partial(_xffn_kernel, tiles_a),
        grid=(n // tm,),
        in_specs=[
            pl.BlockSpec((tm, D_MODEL), row), pl.BlockSpec((tm, D_MODEL), row),
            pl.BlockSpec((n_mem, 2 * D_MODEL), lambda i: (i // tiles_per_seq, 0)),
            resident((D_MODEL, D_MODEL)), vec, vec,
            resident((D_MODEL, 2 * D_FF)), resident((D_FF, D_MODEL)), vec,
        ],
        out_specs=list(_group_specs(tm, D_MODEL, tiles_a)),
        out_shape=[jax.ShapeDtypeStruct((n_a, D_MODEL), F32), jax.ShapeDtypeStruct((n - n_a, D_MODEL), F32)],
        compiler_params=_params("arbitrary"),
        name="xffn",
    )(x1, qx, kv, wo, lnxpost, lnffn, wi, wo2, lnpost)


def _rope_tables(seq):
    rows = seq // GRID_W
    row = jnp.broadcast_to(jnp.arange(rows)[:, None], (rows, GRID_W)).reshape(-1).astype(F32)
    col = jnp.broadcast_to(jnp.arange(GRID_W)[None, :], (rows, GRID_W)).reshape(-1).astype(F32)
    half = ATT_HD // 2
    inv_freq = ROPE_THETA ** (-jnp.arange(0, half, 2, dtype=F32) / half)
    ang_r = row[:, None] * inv_freq[None, :]
    ang_c = col[:, None] * inv_freq[None, :]
    cr, sr, cc, sc = jnp.cos(ang_r), jnp.sin(ang_r), jnp.cos(ang_c), jnp.sin(ang_c)
    return (jnp.concatenate([cr, cr, cc, cc], axis=-1),
            jnp.concatenate([-sr, sr, -sc, sc], axis=-1))


def _pack_w_in(w_in):
    z_lo = C_AQ
    z_hi = z_lo + 2 * GLA_RANK
    assert w_in.shape[1] - 2 * GLA_RANK == C_Z
    w = w_in.astype(BF16)
    pad = jnp.zeros((D_MODEL, LANE - 2 * GLA_RANK), BF16)
    return w, w[:, z_hi:], jnp.concatenate([w[:, z_lo:z_hi], pad], axis=1)


def _pack_decay(wa_f, ba_f, wa_b, ba_b):
    w = jnp.zeros((LANE, 2 * GLA_QK), F32)
    w = w.at[:GLA_RANK, :GLA_QK].set(wa_f).at[GLA_RANK:2 * GLA_RANK, GLA_QK:].set(wa_b)
    return w.astype(BF16), jnp.concatenate([ba_f, ba_b])[None, :]


def _layer(xa, xb, mem_a, mem_b, seq, n_mem, ln_mix_pre, w_in, gla_wa_f, gla_ba_f, gla_wa_b, gla_ba_b, gla_norm,
           att_q_norm, att_k_norm, w_branch_gla, w_branch_att, w_out, ln_mix_post,
           ln_x_pre, ln_mem, x_wq, x_wkv, x_wo, ln_x_post, ln_ffn_pre, ffn_wi, ffn_wo, ln_ffn_post):
    vec = lambda v: v[None, :]
    cos, sin = _rope_tables(seq)
    wdec, bdec = _pack_decay(gla_wa_f, gla_ba_f, gla_wa_b, gla_ba_b)
    gqk, gv, gr, aq, ak, av, gate, z = _inproj(
        xa, xb, vec(ln_mix_pre), _pack_w_in(w_in), vec(att_q_norm), vec(att_k_norm), cos, sin, seq)
    og = _gla(gqk, gv, gr, z, wdec, bdec, vec(gla_norm), seq)
    oa = _attn(aq, ak, av, seq)
    x1, qx = _mix(xa, xb, og, oa, gate, w_branch_gla.astype(BF16), w_branch_att.astype(BF16),
                  w_out.astype(BF16), vec(ln_mix_post), vec(ln_x_pre), x_wq.astype(BF16))
    kv = _memkv(mem_a, mem_b, vec(ln_mem), x_wkv.astype(BF16))
    return _xffn(x1, qx, kv, x_wo.astype(BF16), vec(ln_x_post), vec(ln_ffn_pre), ffn_wi.astype(BF16),
                 ffn_wo.astype(BF16), vec(ln_ffn_post), seq, n_mem, xa.shape[0])


def kernel(x_prompt, x_sample, mem_prompt, mem_sample, ln_mix_pre, w_in, gla_wa_f, gla_ba_f, gla_wa_b, gla_ba_b, gla_norm, att_q_norm, att_k_norm, w_branch_gla, w_branch_att, w_out, ln_mix_post, ln_x_pre, ln_mem, x_wq, x_wkv, x_wo, ln_x_post, ln_ffn_pre, ffn_wi, ffn_wo, ln_ffn_post):
    params = (ln_mix_pre, w_in, gla_wa_f, gla_ba_f, gla_wa_b, gla_ba_b, gla_norm,
              att_q_norm, att_k_norm, w_branch_gla, w_branch_att, w_out, ln_mix_post,
              ln_x_pre, ln_mem, x_wq, x_wkv, x_wo, ln_x_post,
              ln_ffn_pre, ffn_wi, ffn_wo, ln_ffn_post)
    bp, seq, d = x_prompt.shape
    bs = x_sample.shape[0]
    n_mem = mem_prompt.shape[1]
    assert x_sample.shape[1:] == (seq, d) and mem_sample.shape[1:] == (n_mem, d) and d == D_MODEL
    xa = x_prompt.reshape(bp * seq, d)
    xb = x_sample.reshape(bs * seq, d)
    mem_a = mem_prompt.reshape(bp * n_mem, d)
    mem_b = mem_sample.reshape(bs * n_mem, d)
    for l in range(ln_mix_pre.shape[0]):
        xa, xb = _layer(xa, xb, mem_a, mem_b, seq, n_mem, *[p[l] for p in params])
    return (xa.reshape(bp, seq, d), xb.reshape(bs, seq, d))
```
